```python
import math
import jax, jax.numpy as jnp
from jax import lax
import numpy as np

D_MODEL = 2048
BATCH = 4
SEQ = 2048
DEPTH = 4
DEC_BATCH = 128
DEC_SEQ = 8
PAST_LEN = 16384
PAGE_SIZE = 128

N_MIXERS = 2
N_GDN = (DEPTH + 1) // 2
N_MLSTM = DEPTH // 2
GDN_K_HEADS = 16
GDN_V_HEADS = 32
GDN_DK = 128
GDN_DV = 128
GDN_QK_DIM = GDN_K_HEADS * GDN_DK
GDN_V_DIM = GDN_V_HEADS * GDN_DV
GDN_CONV_DIM = 2 * GDN_QK_DIM + GDN_V_DIM
GDN_PROJ_DIM = GDN_CONV_DIM + GDN_V_DIM + 2 * GDN_V_HEADS
GDN_CONV_W = 4
GDN_CHUNK = 64
ML_HEADS = 8
ML_DQK = 128
ML_DV = 256
ML_QK_DIM = ML_HEADS * ML_DQK
ML_V_DIM = ML_HEADS * ML_DV
ML_PROJ_DIM = 2 * ML_QK_DIM + 2 * ML_V_DIM + 2 * ML_HEADS
ML_CHUNK = 64
PEER_HEADS = 8
PEER_N_KEYS = 128
PEER_N_EXPERTS = PEER_N_KEYS * PEER_N_KEYS
PEER_DQ = 256
PEER_TOPK = 16
PEER_BLOCK = 128
DEEPNORM_ALPHA = (2 * DEPTH) ** 0.25
DEEPNORM_BETA = (8 * DEPTH) ** -0.25
LN_EPS = 1e-5
RMS_EPS = 1e-6

kernel_name = 'hybrid_gdn_mlstm_peer_decoder_step'


def layer_norm(x, g, b):
    xf = x.astype(jnp.float32)
    mu = jnp.mean(xf, axis=-1, keepdims=True)
    var = jnp.mean(jnp.square(xf - mu), axis=-1, keepdims=True)
    return ((xf - mu) * lax.rsqrt(var + LN_EPS)).astype(x.dtype) * g + b


def rms_norm(x, w):
    return x * lax.rsqrt(jnp.mean(jnp.square(x), axis=-1, keepdims=True) + RMS_EPS) * w.astype(jnp.float32)


def l2norm(x):
    return x * lax.rsqrt(jnp.sum(jnp.square(x), axis=-1, keepdims=True) + RMS_EPS)


def to_chunks(a, c, pad_value=0.0):
    b, t = a.shape[:2]
    n = -(-t // c)
    a = jnp.pad(a, [(0, 0), (0, n * c - t)] + [(0, 0)] * (a.ndim - 2), constant_values=pad_value)
    a = a.reshape((b, n, c) + a.shape[2:]).swapaxes(2, 3)
    return jnp.moveaxis(a, 1, 0)


def from_chunks(o, t):
    n, b, h, c, e = o.shape
    return jnp.moveaxis(o, 0, 1).swapaxes(2, 3).reshape(b, n * c, h, e)[:, :t]


def causal_conv(x, buf, w):
    t = x.shape[1]
    xp = jnp.concatenate([buf.astype(x.dtype), x], axis=1)
    y = sum(xp[:, i:i + t] * w[i] for i in range(GDN_CONV_W))
    return y, xp[:, t:]


def gated_delta_rule(q, k, v, g, beta, s0):
    t = q.shape[1]
    dv = v.shape[-1]
    c = min(GDN_CHUNK, t)
    qc, kc, vc, gc, bc = (to_chunks(a, c) for a in (q, k, v, g, beta))
    gc = jnp.cumsum(gc, axis=-1)
    idx = jnp.arange(c)
    strict = idx[:, None] > idx[None, :]
    causal = idx[:, None] >= idx[None, :]
    diff = gc[..., :, None] - gc[..., None, :]
    dec_strict = jnp.exp(jnp.where(strict, diff, -jnp.inf))
    dec_causal = jnp.exp(jnp.where(causal, diff, -jnp.inf))
    lmat = bc[..., :, None] * jnp.einsum('nbhid,nbhjd->nbhij', kc, kc) * dec_strict
    rhs = jnp.concatenate([vc * bc[..., None], kc * (bc * jnp.exp(gc))[..., None]], axis=-1)
    sol = lax.linalg.triangular_solve(lmat, rhs, left_side=True, lower=True, unit_diagonal=True)
    u, w = sol[..., :dv], sol[..., dv:]
    qk = jnp.einsum('nbhid,nbhjd->nbhij', qc, kc) * dec_causal

    def step(s, inp):
        q_c, k_c, u_c, w_c, g_c, qk_c = inp
        v_new = u_c - jnp.einsum('bhcd,bhde->bhce', w_c, s)
        o = (jnp.einsum('bhcd,bhde->bhce', q_c * jnp.exp(g_c)[..., None], s)
             + jnp.einsum('bhij,bhje->bhie', qk_c, v_new))
        g_last = g_c[..., -1:]
        s = (s * jnp.exp(g_last)[..., None]
             + jnp.einsum('bhcd,bhce->bhde', k_c * jnp.exp(g_last - g_c)[..., None], v_new))
        return s, o

    s, o = lax.scan(step, s0, (qc, kc, u, w, gc, qk))
    return from_chunks(o, t), s


def gdn_mixer(x, conv_buf, s0, w_in, conv_w, a_log, dt_bias, norm_w, w_out):
    b, t, _ = x.shape
    proj = jnp.einsum('btd,de->bte', x, w_in)
    o1 = GDN_CONV_DIM
    o2 = o1 + GDN_V_DIM
    o3 = o2 + GDN_V_HEADS
    qkv, z, b_pre, a_pre = proj[..., :o1], proj[..., o1:o2], proj[..., o2:o3], proj[..., o3:]
    qkv_c, new_buf = causal_conv(qkv, conv_buf, conv_w)
    qkv_c = jax.nn.silu(qkv_c).astype(jnp.float32)
    rep = GDN_V_HEADS // GDN_K_HEADS
    q = l2norm(qkv_c[..., :GDN_QK_DIM].reshape(b, t, GDN_K_HEADS, GDN_DK)) * GDN_DK ** -0.5
    k = l2norm(qkv_c[..., GDN_QK_DIM:2 * GDN_QK_DIM].reshape(b, t, GDN_K_HEADS, GDN_DK))
    q = jnp.repeat(q, rep, axis=2)
    k = jnp.repeat(k, rep, axis=2)
    v = qkv_c[..., 2 * GDN_QK_DIM:].reshape(b, t, GDN_V_HEADS, GDN_DV)
    beta = jax.nn.sigmoid(b_pre.astype(jnp.float32))
    g = -jnp.exp(a_log.astype(jnp.float32)) * jax.nn.softplus(a_pre.astype(jnp.float32) + dt_bias.astype(jnp.float32))
    o, s = gated_delta_rule(q, k, v, g, beta, s0.astype(jnp.float32))
    o = rms_norm(o, norm_w) * jax.nn.silu(z.reshape(b, t, GDN_V_HEADS, GDN_DV).astype(jnp.float32))
    y = jnp.einsum('bte,ed->btd', o.reshape(b, t, GDN_V_DIM).astype(x.dtype), w_out)
    return y, new_buf.astype(conv_buf.dtype), s.astype(s0.dtype)


def mlstm_cell(q, k, v, ig, lf, c0, n0, m0):
    t = q.shape[1]
    c = min(ML_CHUNK, t)
    qc, kc, vc, lfc = (to_chunks(a, c) for a in (q, k, v, lf))
    igc = to_chunks(ig, c, pad_value=-jnp.inf)
    idx = jnp.arange(c)
    causal = idx[:, None] >= idx[None, :]

    def step(carry, inp):
        cm, nv, m = carry
        q_c, k_c, v_c, i_c, f_c = inp
        bcum = jnp.cumsum(f_c, axis=-1)
        d = jnp.where(causal, bcum[..., :, None] - bcum[..., None, :] + i_c[..., None, :], -jnp.inf)
        inter = bcum + m[..., None]
        m_t = jnp.maximum(inter, jnp.max(d, axis=-1))
        w_intra = jnp.exp(d - m_t[..., None])
        w_inter = jnp.exp(inter - m_t)
        sqk = jnp.einsum('bhid,bhjd->bhij', q_c, k_c) * w_intra
        num = (w_inter[..., None] * jnp.einsum('bhld,bhde->bhle', q_c, cm)
               + jnp.einsum('bhij,bhje->bhie', sqk, v_c))
        den = w_inter * jnp.einsum('bhld,bhd->bhl', q_c, nv) + jnp.sum(sqk, axis=-1)
        h = num / jnp.maximum(jnp.abs(den), jnp.exp(-m_t))[..., None]
        b_last = bcum[..., -1]
        d_end = b_last[..., None] - bcum + i_c
        m_new = jnp.maximum(b_last + m, jnp.max(d_end, axis=-1))
        w_end = jnp.exp(d_end - m_new[..., None])
        scale = jnp.exp(b_last + m - m_new)
        cm = scale[..., None, None] * cm + jnp.einsum('bhl,bhld,bhle->bhde', w_end, k_c, v_c)
        nv = scale[..., None] * nv + jnp.einsum('bhl,bhld->bhd', w_end, k_c)
        return (cm, nv, m_new), h

    (cm, nv, m), h = lax.scan(step, (c0, n0, m0), (qc, kc, vc, igc, lfc))
    return from_chunks(h, t), cm, nv, m


def mlstm_mixer(x, c0, n0, m0, w_in, gate_b, norm_w, w_out):
    b, t, _ = x.shape
    proj = jnp.einsum('btd,de->bte', x, w_in)
    f32 = jnp.float32
    q = proj[..., :ML_QK_DIM].reshape(b, t, ML_HEADS, ML_DQK).astype(f32)
    k = proj[..., ML_QK_DIM:2 * ML_QK_DIM].reshape(b, t, ML_HEADS, ML_DQK).astype(f32) * ML_DQK ** -0.5
    v = proj[..., 2 * ML_QK_DIM:2 * ML_QK_DIM + ML_V_DIM].reshape(b, t, ML_HEADS, ML_DV).astype(f32)
    o_pre = proj[..., 2 * ML_QK_DIM + ML_V_DIM:2 * ML_QK_DIM + 2 * ML_V_DIM]
    gates = (proj[..., 2 * ML_QK_DIM + 2 * ML_V_DIM:] + gate_b).astype(f32)
    ig = gates[..., :ML_HEADS]
    lf = jax.nn.log_sigmoid(gates[..., ML_HEADS:])
    h, cm, nv, m = mlstm_cell(q, k, v, ig, lf, c0.astype(f32), n0.astype(f32), m0.astype(f32))
    h = rms_norm(h, norm_w).reshape(b, t, ML_V_DIM) * jax.nn.sigmoid(o_pre.astype(f32))
    y = jnp.einsum('bte,ed->btd', h.astype(x.dtype), w_out)
    return y, cm.astype(c0.dtype), nv.astype(n0.dtype), m.astype(m0.dtype)


def peer(x, w_q, sub_keys, u_tab, v_tab):
    b, t, d = x.shape
    ntok = b * t
    blk = min(PEER_BLOCK, ntok)
    nb = -(-ntok // blk)
    xt = jnp.pad(x.reshape(ntok, d), ((0, nb * blk - ntok), (0, 0))).reshape(nb, blk, d)
    kk = PEER_TOPK * PEER_TOPK

    def one_block(xb):
        q = jnp.einsum('td,de->te', xb, w_q).reshape(blk, PEER_HEADS, 2, PEER_DQ // 2)
        s = jnp.einsum('thpd,hpnd->thpn', q, sub_keys)
        s_top, i_top = lax.top_k(s, PEER_TOPK)
        cand_s = (s_top[:, :, 0, :, None] + s_top[:, :, 1, None, :]).reshape(blk, PEER_HEADS, kk)
        cand_i = (i_top[:, :, 0, :, None] * PEER_N_KEYS + i_top[:, :, 1, None, :]).reshape(blk, PEER_HEADS, kk)
        best_s, best_j = lax.top_k(cand_s, PEER_TOPK)
        expert = jnp.take_along_axis(cand_i, best_j, axis=-1).reshape(blk, PEER_HEADS * PEER_TOPK)
        gate = jax.nn.softmax(best_s.astype(jnp.float32), axis=-1).reshape(blk, PEER_HEADS * PEER_TOPK)
        u = jnp.take(u_tab, expert, axis=0)
        v = jnp.take(v_tab, expert, axis=0)
        act = jax.nn.gelu(jnp.einsum('ted,td->te', u, xb)) * gate.astype(xb.dtype)
        return jnp.einsum('te,ted->td', act, v)

    y = lax.map(one_block, xt)
    return y.reshape(nb * blk, d)[:ntok].reshape(b, t, d)


def trunk(x, conv_buf, gdn_s, ml_c, ml_n, ml_m,
          gdn_w_in, gdn_conv_w, gdn_a_log, gdn_dt_bias, gdn_norm_w, gdn_w_out,
          ml_w_in, ml_gate_b, ml_norm_w, ml_w_out,
          ln_mix_g, ln_mix_b, ln_ffn_g, ln_ffn_b,
          peer_w_q, peer_keys, peer_u, peer_v):
    new_conv, new_s, new_c, new_n, new_m = [], [], [], [], []
    for layer in range(DEPTH):
        j = layer // N_MIXERS
        if layer % N_MIXERS == 0:
            mix, buf, s = gdn_mixer(x, conv_buf[j], gdn_s[j], gdn_w_in[j], gdn_conv_w[j],
                                    gdn_a_log[j], gdn_dt_bias[j], gdn_norm_w[j], gdn_w_out[j])
            new_conv.append(buf)
            new_s.append(s)
        else:
            mix, cm, nv, m = mlstm_mixer(x, ml_c[j], ml_n[j], ml_m[j], ml_w_in[j], ml_gate_b[j],
                                         ml_norm_w[j], ml_w_out[j])
            new_c.append(cm)
            new_n.append(nv)
            new_m.append(m)
        x = layer_norm(DEEPNORM_ALPHA * x + mix, ln_mix_g[layer], ln_mix_b[layer])
        x = layer_norm(DEEPNORM_ALPHA * x + peer(x, peer_w_q[layer], peer_keys[layer], peer_u[layer], peer_v[layer]),
                       ln_ffn_g[layer], ln_ffn_b[layer])
    return x, jnp.stack(new_conv), jnp.stack(new_s), jnp.stack(new_c), jnp.stack(new_n), jnp.stack(new_m)


def setup_inputs(seed: int = 0) -> dict:
    key = jax.random.key(seed)
    ks = jax.random.split(key, 32)
    f32 = jnp.float32

    def nrm(k, shape, scale):
        return jax.random.normal(k, shape, f32) * scale

    x_prompt = nrm(ks[0], (BATCH, SEQ, D_MODEL), 1.0)
    x_sample = nrm(ks[1], (DEC_BATCH, DEC_SEQ, D_MODEL), 1.0)
    state_gdn_conv = nrm(ks[2], (N_GDN, DEC_BATCH, GDN_CONV_W - 1, GDN_CONV_DIM), 1.0)
    state_gdn_s = nrm(ks[3], (N_GDN, DEC_BATCH, GDN_V_HEADS, GDN_DK, GDN_DV), 0.05)
    state_mlstm_c = nrm(ks[4], (N_MLSTM, DEC_BATCH, ML_HEADS, ML_DQK, ML_DV), 0.3)
    state_mlstm_n = nrm(ks[5], (N_MLSTM, DEC_BATCH, ML_HEADS, ML_DQK), 0.3)
    state_mlstm_m = jax.random.uniform(ks[6], (N_MLSTM, DEC_BATCH, ML_HEADS), f32, 1.0, 4.0)

    gdn_w_in = nrm(ks[7], (N_GDN, D_MODEL, GDN_PROJ_DIM), D_MODEL ** -0.5)
    gdn_conv_w = nrm(ks[8], (N_GDN, GDN_CONV_W, GDN_CONV_DIM), GDN_CONV_W ** -0.5)
    gdn_a_log = jnp.log(jax.random.uniform(ks[9], (N_GDN, GDN_V_HEADS), f32, 1.0, 16.0))
    dt = jnp.exp(jax.random.uniform(ks[10], (N_GDN, GDN_V_HEADS), f32, math.log(1e-3), math.log(1e-1)))
    gdn_dt_bias = dt + jnp.log(-jnp.expm1(-dt))
    gdn_norm_w = 1.0 + nrm(ks[11], (N_GDN, GDN_DV), 0.02)
    gdn_w_out = nrm(ks[12], (N_GDN, GDN_V_DIM, D_MODEL), GDN_V_DIM ** -0.5 * DEEPNORM_BETA)

    ml_w_in = nrm(ks[13], (N_MLSTM, D_MODEL, ML_PROJ_DIM), D_MODEL ** -0.5)
    input_b = nrm(ks[14], (N_MLSTM, ML_HEADS), 0.1)
    forget_b = jnp.linspace(3.0, 6.0, ML_HEADS, dtype=f32)[None, :] + nrm(ks[15], (N_MLSTM, ML_HEADS), 0.1)
    ml_gate_b = jnp.concatenate([input_b, forget_b], axis=-1)
    ml_norm_w = 1.0 + nrm(ks[16], (N_MLSTM, ML_DV), 0.02)
    ml_w_out = nrm(ks[17], (N_MLSTM, ML_V_DIM, D_MODEL), ML_V_DIM ** -0.5 * DEEPNORM_BETA)

    ln_mix_g = 1.0 + nrm(ks[18], (DEPTH, D_MODEL), 0.02)
    ln_mix_b = nrm(ks[19], (DEPTH, D_MODEL), 0.02)
    ln_ffn_g = 1.0 + nrm(ks[20], (DEPTH, D_MODEL), 0.02)
    ln_ffn_b = nrm(ks[21], (DEPTH, D_MODEL), 0.02)

    peer_w_q = nrm(ks[22], (DEPTH, D_MODEL, PEER_HEADS * PEER_DQ), D_MODEL ** -0.5)
    peer_keys = nrm(ks[23], (DEPTH, PEER_HEADS, 2, PEER_N_KEYS, PEER_DQ // 2), (PEER_DQ // 2) ** -0.5)
    peer_u = nrm(ks[24], (DEPTH, PEER_N_EXPERTS, D_MODEL), D_MODEL ** -0.5)
    peer_v = nrm(ks[25], (DEPTH, PEER_N_EXPERTS, D_MODEL), (PEER_HEADS * PEER_TOPK) ** -0.5 * DEEPNORM_BETA)

    return {
        'x_prompt': x_prompt, 'x_sample': x_sample,
        'state_gdn_conv': state_gdn_conv, 'state_gdn_s': state_gdn_s,
        'state_mlstm_c': state_mlstm_c, 'state_mlstm_n': state_mlstm_n, 'state_mlstm_m': state_mlstm_m,
        'gdn_w_in': gdn_w_in, 'gdn_conv_w': gdn_conv_w, 'gdn_a_log': gdn_a_log, 'gdn_dt_bias': gdn_dt_bias,
        'gdn_norm_w': gdn_norm_w, 'gdn_w_out': gdn_w_out,
        'ml_w_in': ml_w_in, 'ml_gate_b': ml_gate_b, 'ml_norm_w': ml_norm_w, 'ml_w_out': ml_w_out,
        'ln_mix_g': ln_mix_g, 'ln_mix_b': ln_mix_b, 'ln_ffn_g': ln_ffn_g, 'ln_ffn_b': ln_ffn_b,
        'peer_w_q': peer_w_q, 'peer_keys': peer_keys, 'peer_u': peer_u, 'peer_v': peer_v,
    }


def reference(x_prompt, x_sample, state_gdn_conv, state_gdn_s, state_mlstm_c, state_mlstm_n, state_mlstm_m,
              gdn_w_in, gdn_conv_w, gdn_a_log, gdn_dt_bias, gdn_norm_w, gdn_w_out,
              ml_w_in, ml_gate_b, ml_norm_w, ml_w_out,
              ln_mix_g, ln_mix_b, ln_ffn_g, ln_ffn_b,
              peer_w_q, peer_keys, peer_u, peer_v):
    weights = (gdn_w_in, gdn_conv_w, gdn_a_log, gdn_dt_bias, gdn_norm_w, gdn_w_out,
               ml_w_in, ml_gate_b, ml_norm_w, ml_w_out,
               ln_mix_g, ln_mix_b, ln_ffn_g, ln_ffn_b,
               peer_w_q, peer_keys, peer_u, peer_v)
    bp = x_prompt.shape[0]
    z_conv = jnp.zeros((N_GDN, bp) + state_gdn_conv.shape[2:], state_gdn_conv.dtype)
    z_s = jnp.zeros((N_GDN, bp) + state_gdn_s.shape[2:], state_gdn_s.dtype)
    z_c = jnp.zeros((N_MLSTM, bp) + state_mlstm_c.shape[2:], state_mlstm_c.dtype)
    z_n = jnp.zeros((N_MLSTM, bp) + state_mlstm_n.shape[2:], state_mlstm_n.dtype)
    z_m = jnp.zeros((N_MLSTM, bp) + state_mlstm_m.shape[2:], state_mlstm_m.dtype)
    y_prompt, p_conv, p_s, p_c, p_n, p_m = trunk(x_prompt, z_conv, z_s, z_c, z_n, z_m, *weights)
    y_sample, s_conv, s_s, s_c, s_n, s_m = trunk(x_sample, state_gdn_conv, state_gdn_s, state_mlstm_c,
                                                 state_mlstm_n, state_mlstm_m, *weights)
    return (y_prompt, y_sample, p_conv, p_s, p_c, p_n, p_m, s_conv, s_s, s_c, s_n, s_m)
```

```python
import functools

import jax
import jax.numpy as jnp
from jax import lax
from jax.experimental import pallas as pl
from jax.experimental.pallas import tpu as pltpu

F32 = jnp.float32
BF16 = jnp.bfloat16

D_MODEL = 2048
DEPTH = 4
GDN_K_HEADS = 16
GDN_V_HEADS = 32
GDN_DK = 128
GDN_DV = 128
GDN_QK_DIM = GDN_K_HEADS * GDN_DK
GDN_V_DIM = GDN_V_HEADS * GDN_DV
GDN_CONV_DIM = 2 * GDN_QK_DIM + GDN_V_DIM
GDN_MAIN_DIM = GDN_CONV_DIM + GDN_V_DIM
GDN_CONV_W = 4
GDN_CHUNK = 64
ML_HEADS = 8
ML_DQK = 128
ML_DV = 256
ML_QK_DIM = ML_HEADS * ML_DQK
ML_V_DIM = ML_HEADS * ML_DV
ML_MAIN_DIM = 2 * ML_QK_DIM + 2 * ML_V_DIM
ML_CHUNK = 64
PEER_HEADS = 8
PEER_N_KEYS = 128
PEER_HALF = 128
PEER_TOPK = 16
DEEPNORM_ALPHA = (2 * DEPTH) ** 0.25
LN_EPS = 1e-5
RMS_EPS = 1e-6
NEG_BIG = -1e30
NEG_INF = float("-inf")

VMEM_LIMIT_BYTES = 56 * 1024 * 1024
HIGHEST = lax.Precision.HIGHEST


def _cparams(*sem):
    return pltpu.CompilerParams(dimension_semantics=sem, vmem_limit_bytes=VMEM_LIMIT_BYTES)


def _dot(a, b):
    return jnp.dot(a.astype(BF16), b.astype(BF16), preferred_element_type=F32)


def _dot_nt(a, b):
    return lax.dot_general(a.astype(BF16), b.astype(BF16), (((1,), (1,)), ((), ())),
                           preferred_element_type=F32)


def _dot_tn(a, b):
    return lax.dot_general(a.astype(BF16), b.astype(BF16), (((0,), (0,)), ((), ())),
                           preferred_element_type=F32)


def _dot_f32(a, b):
    return jnp.dot(a, b, preferred_element_type=F32, precision=HIGHEST)


def _sigmoid(x):
    return 1.0 / (1.0 + jnp.exp(-x))


def _silu(x):
    return x * _sigmoid(x)


def _softplus(x):
    return jnp.maximum(x, 0.0) + jnp.log(1.0 + jnp.exp(-jnp.abs(x)))


def _log_sigmoid(x):
    return -_softplus(-x)


def _gelu_tanh(x):
    return 0.5 * x * (1.0 + jnp.tanh(0.7978845608028654 * (x + 0.044715 * (x * x * x))))


def _lower(c, strict):
    r = lax.broadcasted_iota(jnp.int32, (c, c), 0)
    k = lax.broadcasted_iota(jnp.int32, (c, c), 1)
    return (r > k) if strict else (r >= k)


def _upper(c):
    r = lax.broadcasted_iota(jnp.int32, (c, c), 0)
    k = lax.broadcasted_iota(jnp.int32, (c, c), 1)
    return r <= k


def _unit_lower_inverse(l, c):
    eye = (lax.broadcasted_iota(jnp.int32, (c, c), 0) == lax.broadcasted_iota(jnp.int32, (c, c), 1)).astype(F32)
    inv = eye - l
    power = l
    span = 2
    while span < c:
        power = _dot_f32(power, power)
        inv = inv + _dot_f32(inv, power)
        span *= 2
    return inv


def _layer_norm_rows(v, g, b):
    mu = jnp.mean(v, axis=-1, keepdims=True)
    d = v - mu
    var = jnp.mean(d * d, axis=-1, keepdims=True)
    return d * lax.rsqrt(var + LN_EPS) * g + b


def _matmul_kernel(x_ref, w_ref, o_ref):
    o_ref[...] = _dot(x_ref[...], w_ref[...])


def matmul(x, w, n_cols, *, tm, tn):
    m, k = x.shape
    return pl.pallas_call(
        _matmul_kernel,
        grid=(m // tm, n_cols // tn),
        in_specs=[pl.BlockSpec((tm, k), lambda i, j: (i, 0)),
                  pl.BlockSpec((k, tn), lambda i, j: (0, j))],
        out_specs=pl.BlockSpec((tm, tn), lambda i, j: (i, j)),
        out_shape=jax.ShapeDtypeStruct((m, n_cols), F32),
        compiler_params=_cparams("parallel", "arbitrary"),
        name="proj_matmul",
    )(x, w)


def _gate_proj_kernel(x_ref, w_ref, wt_ref, o_ref, ot_ref):
    x = x_ref[...]
    o_ref[...] = _dot(x, w_ref[...])
    ot_ref[...] = _dot_nt(wt_ref[...], x)


def gate_proj(x, w_gate, *, tm):
    m, k = x.shape
    n = w_gate.shape[1]
    return pl.pallas_call(
        _gate_proj_kernel,
        grid=(m // tm,),
        in_specs=[pl.BlockSpec((tm, k), lambda i: (i, 0)),
                  pl.BlockSpec((k, n), lambda i: (0, 0)),
                  pl.BlockSpec((n, k), lambda i: (0, 0))],
        out_specs=[pl.BlockSpec((tm, n), lambda i: (i, 0)),
                   pl.BlockSpec((n, tm), lambda i: (0, i))],
        out_shape=[jax.ShapeDtypeStruct((m, n), F32), jax.ShapeDtypeStruct((n, m), F32)],
        compiler_params=_cparams("parallel"),
        name="gate_proj",
    )(x, w_gate, w_gate.T)


def _out_proj_ln_kernel(h_ref, w_ref, x_ref, g_ref, b_ref, o_ref, ob_ref, acc_ref):
    kk = pl.program_id(1)

    @pl.when(kk == 0)
    def _():
        acc_ref[...] = jnp.zeros_like(acc_ref)

    acc_ref[...] += _dot(h_ref[...], w_ref[...])

    @pl.when(kk == pl.num_programs(1) - 1)
    def _():
        y = _layer_norm_rows(DEEPNORM_ALPHA * x_ref[...] + acc_ref[...], g_ref[...], b_ref[...])
        o_ref[...] = y
        ob_ref[...] = y.astype(BF16)


def out_proj_ln(h, w, x, g, b, *, tm, tk):
    m, k = h.shape
    d = w.shape[1]
    return pl.pallas_call(
        _out_proj_ln_kernel,
        grid=(m // tm, k // tk),
        in_specs=[pl.BlockSpec((tm, tk), lambda i, j: (i, j)),
                  pl.BlockSpec((tk, d), lambda i, j: (j, 0)),
                  pl.BlockSpec((tm, d), lambda i, j: (i, 0)),
                  pl.BlockSpec((1, d), lambda i, j: (0, 0)),
                  pl.BlockSpec((1, d), lambda i, j: (0, 0))],
        out_specs=[pl.BlockSpec((tm, d), lambda i, j: (i, 0)),
                   pl.BlockSpec((tm, d), lambda i, j: (i, 0))],
        out_shape=[jax.ShapeDtypeStruct((m, d), F32), jax.ShapeDtypeStruct((m, d), BF16)],
        scratch_shapes=[pltpu.VMEM((tm, d), F32)],
        compiler_params=_cparams("parallel", "arbitrary"),
        name="out_proj_ln",
    )(h, w, x, g.reshape(1, d), b.reshape(1, d))


CONV_PAD = 8


def _gdn_kernel(qkv_ref, z_ref, bpre_ref, apre_ref, bpre_t_ref, apre_t_ref, conv0_ref, s0_ref,
                convw_ref, alog_ref, dtb_ref, alog_t_ref, dtb_t_ref, normw_ref,
                o_ref, convo_ref, so_ref,
                xp_ref, s_ref, *, c):
    step = pl.program_id(1)
    hist = GDN_CONV_W - 1

    @pl.when(step == 0)
    def _():
        xp_ref[CONV_PAD - hist:CONV_PAD, :] = conv0_ref[0]
        s_ref[...] = s0_ref[0]

    x = qkv_ref[0]
    xp_ref[CONV_PAD:CONV_PAD + c, :] = x
    y = xp_ref[CONV_PAD - hist:CONV_PAD - hist + c, :] * convw_ref[0:1, :]
    for i in range(1, GDN_CONV_W):
        y = y + xp_ref[CONV_PAD - hist + i:CONV_PAD - hist + i + c, :] * convw_ref[i:i + 1, :]
    tail = x[c - hist:c, :]
    xp_ref[CONV_PAD - hist:CONV_PAD, :] = tail
    convo_ref[0] = tail
    xp_ref[CONV_PAD:CONV_PAD + c, :] = _silu(y)

    beta_c = _sigmoid(bpre_ref[0])
    g_c = -jnp.exp(alog_ref[...]) * _softplus(apre_ref[0] + dtb_ref[...])
    g_r = -jnp.exp(alog_t_ref[...]) * _softplus(apre_t_ref[0] + dtb_t_ref[...])
    causal = _lower(c, strict=False)
    strict = _lower(c, strict=True)
    gc_c = _dot_f32(causal.astype(F32), g_c)
    gc_r = _dot_f32(g_r, _upper(c).astype(F32))

    rep = GDN_V_HEADS // GDN_K_HEADS
    for kh in range(GDN_K_HEADS):
        q = xp_ref[CONV_PAD:CONV_PAD + c, kh * GDN_DK:(kh + 1) * GDN_DK]
        k = xp_ref[CONV_PAD:CONV_PAD + c, GDN_QK_DIM + kh * GDN_DK:GDN_QK_DIM + (kh + 1) * GDN_DK]
        q = q * lax.rsqrt(jnp.sum(q * q, axis=-1, keepdims=True) + RMS_EPS) * (GDN_DK ** -0.5)
        k = k * lax.rsqrt(jnp.sum(k * k, axis=-1, keepdims=True) + RMS_EPS)
        kk = _dot_nt(k, k)
        qk = _dot_nt(q, k)
        for h in range(kh * rep, (kh + 1) * rep):
            v = xp_ref[CONV_PAD:CONV_PAD + c, 2 * GDN_QK_DIM + h * GDN_DV:2 * GDN_QK_DIM + (h + 1) * GDN_DV]
            gcol = gc_c[:, h:h + 1]
            grow = gc_r[h:h + 1, :]
            bcol = beta_c[:, h:h + 1]
            diff = gcol - grow
            dec_strict = jnp.exp(jnp.where(strict, diff, NEG_BIG))
            dec_causal = jnp.exp(jnp.where(causal, diff, NEG_BIG))
            inv = _unit_lower_inverse(bcol * kk * dec_strict, c)
            u = _dot_f32(inv, v * bcol)
            w = _dot_f32(inv, k * (bcol * jnp.exp(gcol)))
            s = s_ref[h]
            v_new = u - _dot(w, s)
            o = _dot(q * jnp.exp(gcol), s) + _dot(qk * dec_causal, v_new)
            g_last = gcol[c - 1:c, :]
            s_ref[h] = s * jnp.exp(g_last) + _dot_tn(k * jnp.exp(g_last - gcol), v_new)
            o = o * lax.rsqrt(jnp.mean(o * o, axis=-1, keepdims=True) + RMS_EPS) * normw_ref[...]
            zh = z_ref[0, :, h * GDN_DV:(h + 1) * GDN_DV]
            o_ref[0, :, h * GDN_DV:(h + 1) * GDN_DV] = (o * _silu(zh)).astype(o_ref.dtype)

    @pl.when(step == pl.num_programs(1) - 1)
    def _():
        so_ref[0] = s_ref[...]


def gdn_group(proj, gates, gates_t, conv0, s0, conv_w, a_log, dt_bias, norm_w, *, row0, nb, t, c):
    tokens = proj.shape[0]
    nchunk = t // c
    blk0 = row0 // c
    hv = GDN_V_HEADS
    proj3 = proj.reshape(tokens // c, c, proj.shape[1])
    bpre = gates[:, :hv].reshape(tokens // c, c, hv)
    apre = gates[:, hv:].reshape(tokens // c, c, hv)
    bpre_t = gates_t[:hv].reshape(hv, tokens // c, c).transpose(1, 0, 2)
    apre_t = gates_t[hv:].reshape(hv, tokens // c, c).transpose(1, 0, 2)
    z_blk = GDN_CONV_DIM // GDN_V_DIM

    def rows(b, s):
        return blk0 + b * nchunk + s

    kernel = functools.partial(_gdn_kernel, c=c)
    o, conv, s = pl.pallas_call(
        kernel,
        grid=(nb, nchunk),
        in_specs=[
            pl.BlockSpec((1, c, GDN_CONV_DIM), lambda b, s: (rows(b, s), 0, 0)),
            pl.BlockSpec((1, c, GDN_V_DIM), lambda b, s: (rows(b, s), 0, z_blk)),
            pl.BlockSpec((1, c, hv), lambda b, s: (rows(b, s), 0, 0)),
            pl.BlockSpec((1, c, hv), lambda b, s: (rows(b, s), 0, 0)),
            pl.BlockSpec((1, hv, c), lambda b, s: (rows(b, s), 0, 0)),
            pl.BlockSpec((1, hv, c), lambda b, s: (rows(b, s), 0, 0)),
            pl.BlockSpec((1, GDN_CONV_W - 1, GDN_CONV_DIM), lambda b, s: (b, 0, 0)),
            pl.BlockSpec((1, hv, GDN_DK, GDN_DV), lambda b, s: (b, 0, 0, 0)),
            pl.BlockSpec((GDN_CONV_W, GDN_CONV_DIM), lambda b, s: (0, 0)),
            pl.BlockSpec((1, hv), lambda b, s: (0, 0)),
            pl.BlockSpec((1, hv), lambda b, s: (0, 0)),
            pl.BlockSpec((hv, 1), lambda b, s: (0, 0)),
            pl.BlockSpec((hv, 1), lambda b, s: (0, 0)),
            pl.BlockSpec((1, GDN_DV), lambda b, s: (0, 0)),
        ],
        out_specs=[
            pl.BlockSpec((1, c, GDN_V_DIM), lambda b, s: (b * nchunk + s, 0, 0)),
            pl.BlockSpec((1, GDN_CONV_W - 1, GDN_CONV_DIM), lambda b, s: (b, 0, 0)),
            pl.BlockSpec((1, hv, GDN_DK, GDN_DV), lambda b, s: (b, 0, 0, 0)),
        ],
        out_shape=[
            jax.ShapeDtypeStruct((nb * nchunk, c, GDN_V_DIM), BF16),
            jax.ShapeDtypeStruct((nb, GDN_CONV_W - 1, GDN_CONV_DIM), F32),
            jax.ShapeDtypeStruct((nb, hv, GDN_DK, GDN_DV), F32),
        ],
        scratch_shapes=[pltpu.VMEM((CONV_PAD + c, GDN_CONV_DIM), F32),
                        pltpu.VMEM((hv, GDN_DK, GDN_DV), F32)],
        compiler_params=_cparams("parallel", "arbitrary"),
        name=f"gdn_chunk{c}",
    )(proj3, proj3, bpre, apre, bpre_t, apre_t, conv0, s0, conv_w,
      a_log.reshape(1, hv), dt_bias.reshape(1, hv), a_log.reshape(hv, 1), dt_bias.reshape(hv, 1),
      norm_w.reshape(1, GDN_DV))
    return o.reshape(nb * t, GDN_V_DIM), conv, s


def _mlstm_kernel(main_ref, ig_ref, fg_ref, ig_t_ref, fg_t_ref, c0_ref, n0_ref, m0_ref,
                  bi_ref, bf_ref, bi_t_ref, bf_t_ref, normw_ref,
                  o_ref, co_ref, no_ref, mo_ref,
                  c_ref, n_ref, m_ref, *, c):
    step = pl.program_id(1)

    @pl.when(step == 0)
    def _():
        c_ref[...] = c0_ref[0]
        n_ref[...] = n0_ref[0]
        m_ref[...] = jnp.broadcast_to(m0_ref[0], m_ref.shape)

    ig_c = ig_ref[0] + bi_ref[...]
    lf_c = _log_sigmoid(fg_ref[0] + bf_ref[...])
    ig_r = ig_t_ref[0] + bi_t_ref[...]
    lf_r = _log_sigmoid(fg_t_ref[0] + bf_t_ref[...])
    causal = _lower(c, strict=False)
    bc_c = _dot_f32(causal.astype(F32), lf_c)
    bc_r = _dot_f32(lf_r, _upper(c).astype(F32))

    for h in range(ML_HEADS):
        q = main_ref[0, :, h * ML_DQK:(h + 1) * ML_DQK]
        k = main_ref[0, :, ML_QK_DIM + h * ML_DQK:ML_QK_DIM + (h + 1) * ML_DQK] * (ML_DQK ** -0.5)
        v = main_ref[0, :, 2 * ML_QK_DIM + h * ML_DV:2 * ML_QK_DIM + (h + 1) * ML_DV]
        o_pre = main_ref[0, :, 2 * ML_QK_DIM + ML_V_DIM + h * ML_DV:2 * ML_QK_DIM + ML_V_DIM + (h + 1) * ML_DV]
        bcol = bc_c[:, h:h + 1]
        brow = bc_r[h:h + 1, :]
        icol = ig_c[:, h:h + 1]
        irow = ig_r[h:h + 1, :]
        m_prev = m_ref[h:h + 1, 0:1]
        cm = c_ref[h]
        nv = n_ref[h:h + 1, :]

        d = jnp.where(causal, bcol - brow + irow, NEG_BIG)
        inter = bcol + m_prev
        m_t = jnp.maximum(inter, jnp.max(d, axis=-1, keepdims=True))
        w_intra = jnp.exp(d - m_t)
        w_inter = jnp.exp(inter - m_t)
        sqk = _dot_nt(q, k) * w_intra
        num = w_inter * _dot(q, cm) + _dot(sqk, v)
        den = w_inter * jnp.sum(q * nv, axis=-1, keepdims=True) + jnp.sum(sqk, axis=-1, keepdims=True)
        hid = num / jnp.maximum(jnp.abs(den), jnp.exp(-m_t))

        b_last = bcol[c - 1:c, :]
        d_end = b_last - bcol + icol
        m_new = jnp.maximum(b_last + m_prev, jnp.max(d_end, axis=0, keepdims=True))
        wk = jnp.exp(d_end - m_new) * k
        scale = jnp.exp(b_last + m_prev - m_new)
        c_ref[h] = scale * cm + _dot_tn(wk, v)
        n_ref[h:h + 1, :] = scale * nv + jnp.sum(wk, axis=0, keepdims=True)
        m_ref[h:h + 1, :] = jnp.broadcast_to(m_new, (1, m_ref.shape[1]))

        hid = hid * lax.rsqrt(jnp.mean(hid * hid, axis=-1, keepdims=True) + RMS_EPS) * normw_ref[...]
        o_ref[0, :, h * ML_DV:(h + 1) * ML_DV] = (hid * _sigmoid(o_pre)).astype(o_ref.dtype)

    @pl.when(step == pl.num_programs(1) - 1)
    def _():
        co_ref[0] = c_ref[...]
        no_ref[0] = n_ref[...]
        mo_ref[0] = m_ref[:, 0:1]


def mlstm_group(proj, gates, gates_t, c0, n0, m0, gate_b, norm_w, *, row0, nb, t, c):
    tokens = proj.shape[0]
    nchunk = t // c
    blk0 = row0 // c
    nh = ML_HEADS
    proj3 = proj.reshape(tokens // c, c, proj.shape[1])
    ig = gates[:, :nh].reshape(tokens // c, c, nh)
    fg = gates[:, nh:].reshape(tokens // c, c, nh)
    ig_t = gates_t[:nh].reshape(nh, tokens // c, c).transpose(1, 0, 2)
    fg_t = gates_t[nh:].reshape(nh, tokens // c, c).transpose(1, 0, 2)

    def rows(b, s):
        return blk0 + b * nchunk + s

    kernel = functools.partial(_mlstm_kernel, c=c)
    o, cm, nv, m = pl.pallas_call(
        kernel,
        grid=(nb, nchunk),
        in_specs=[
            pl.BlockSpec((1, c, ML_MAIN_DIM), lambda b, s: (rows(b, s), 0, 0)),
            pl.BlockSpec((1, c, nh), lambda b, s: (rows(b, s), 0, 0)),
            pl.BlockSpec((1, c, nh), lambda b, s: (rows(b, s), 0, 0)),
            pl.BlockSpec((1, nh, c), lambda b, s: (rows(b, s), 0, 0)),
            pl.BlockSpec((1, nh, c), lambda b, s: (rows(b, s), 0, 0)),
            pl.BlockSpec((1, nh, ML_DQK, ML_DV), lambda b, s: (b, 0, 0, 0)),
            pl.BlockSpec((1, nh, ML_DQK), lambda b, s: (b, 0, 0)),
            pl.BlockSpec((1, nh, 1), lambda b, s: (b, 0, 0)),
            pl.BlockSpec((1, nh), lambda b, s: (0, 0)),
            pl.BlockSpec((1, nh), lambda b, s: (0, 0)),
            pl.BlockSpec((nh, 1), lambda b, s: (0, 0)),
            pl.BlockSpec((nh, 1), lambda b, s: (0, 0)),
            pl.BlockSpec((1, ML_DV), lambda b, s: (0, 0)),
        ],
        out_specs=[
            pl.BlockSpec((1, c, ML_V_DIM), lambda b, s: (b * nchunk + s, 0, 0)),
            pl.BlockSpec((1, nh, ML_DQK, ML_DV), lambda b, s: (b, 0, 0, 0)),
            pl.BlockSpec((1, nh, ML_DQK), lambda b, s: (b, 0, 0)),
            pl.BlockSpec((1, nh, 1), lambda b, s: (b, 0, 0)),
        ],
        out_shape=[
            jax.ShapeDtypeStruct((nb * nchunk, c, ML_V_DIM), BF16),
            jax.ShapeDtypeStruct((nb, nh, ML_DQK, ML_DV), F32),
            jax.ShapeDtypeStruct((nb, nh, ML_DQK), F32),
            jax.ShapeDtypeStruct((nb, nh, 1), F32),
        ],
        scratch_shapes=[pltpu.VMEM((nh, ML_DQK, ML_DV), F32),
                        pltpu.VMEM((nh, ML_DQK), F32),
                        pltpu.VMEM((nh, 128), F32)],
        compiler_params=_cparams("parallel", "arbitrary"),
        name=f"mlstm_chunk{c}",
    )(proj3, ig, fg, ig_t, fg_t, c0, n0, m0.reshape(nb, nh, 1),
      gate_b[:nh].reshape(1, nh), gate_b[nh:].reshape(1, nh),
      gate_b[:nh].reshape(nh, 1), gate_b[nh:].reshape(nh, 1), norm_w.reshape(1, ML_DV))
    return o.reshape(nb * t, ML_V_DIM), cm, nv, m.reshape(nb, nh)


_CAND_PAIRS = [(a, b) for a in range(PEER_TOPK) for b in range(PEER_TOPK) if (a + 1) * (b + 1) <= PEER_TOPK]
_CAND_ROWS = -(-len(_CAND_PAIRS) // 8) * 8


def _top_values(work, count):
    rows = lax.broadcasted_iota(jnp.int32, work.shape, 0)
    out = []
    for r in range(count):
        m = jnp.max(work, axis=0, keepdims=True)
        out.append(m)
        if r + 1 < count:
            first = jnp.min(jnp.where(work == m, rows, work.shape[0]), axis=0, keepdims=True)
            work = jnp.where(rows == first, NEG_INF, work)
    return out


def _peer_route_kernel(q_ref, keys_ref, s1_ref, s2_ref, tau_ref, lse_ref, cand_ref):
    cand_ref[...] = jnp.full(cand_ref.shape, NEG_INF, F32)
    for h in range(PEER_HEADS):
        tops = []
        for p, s_ref in enumerate((s1_ref, s2_ref)):
            col = (2 * h + p) * PEER_HALF
            s = _dot_nt(keys_ref[h, p], q_ref[:, col:col + PEER_HALF])
            s_ref[h] = s
            tops.append(_top_values(s, PEER_TOPK))
        for i, (a, b) in enumerate(_CAND_PAIRS):
            cand_ref[i:i + 1, :] = tops[0][a] + tops[1][b]
        cand = cand_ref[...]
        tau = _top_values(cand, PEER_TOPK)[-1]
        cmax = tops[0][0] + tops[1][0]
        z = jnp.sum(jnp.where(cand >= tau, jnp.exp(cand - cmax), 0.0), axis=0, keepdims=True)
        tau_ref[h:h + 1, :] = tau
        lse_ref[h:h + 1, :] = cmax + jnp.log(z)


def peer_route(q, keys, *, tm):
    m = q.shape[0]
    nh = PEER_HEADS
    return pl.pallas_call(
        _peer_route_kernel,
        grid=(m // tm,),
        in_specs=[pl.BlockSpec((tm, q.shape[1]), lambda i: (i, 0)),
                  pl.BlockSpec(keys.shape, lambda i: (0, 0, 0, 0))],
        out_specs=[pl.BlockSpec((nh, PEER_N_KEYS, tm), lambda i: (0, 0, i)),
                   pl.BlockSpec((nh, PEER_N_KEYS, tm), lambda i: (0, 0, i)),
                   pl.BlockSpec((nh, tm), lambda i: (0, i)),
                   pl.BlockSpec((nh, tm), lambda i: (0, i))],
        out_shape=[jax.ShapeDtypeStruct((nh, PEER_N_KEYS, m), F32),
                   jax.ShapeDtypeStruct((nh, PEER_N_KEYS, m), F32),
                   jax.ShapeDtypeStruct((nh, m), F32),
                   jax.ShapeDtypeStruct((nh, m), F32)],
        scratch_shapes=[pltpu.VMEM((_CAND_ROWS, tm), F32)],
        compiler_params=_cparams("parallel"),
        name="peer_route",
    )(q, keys)


def _peer_main_kernel(xb_ref, u_ref, vt_ref, s1_ref, s2_ref, tau_ref, lse_ref, x_ref, g_ref, b_ref,
                      o_ref, ob_ref, acc_ref, a_ref, *, te):
    e = pl.program_id(1)
    groups = te // PEER_N_KEYS

    @pl.when(e == 0)
    def _():
        acc_ref[...] = jnp.zeros_like(acc_ref)

    ht = _dot_nt(u_ref[...], xb_ref[...])
    for cc in range(groups):
        key1 = e * groups + cc
        gate = jnp.zeros((PEER_N_KEYS, ht.shape[1]), F32)
        for h in range(PEER_HEADS):
            score = s2_ref[h] + s1_ref[h, pl.ds(key1, 1), :]
            gate = gate + jnp.where(score >= tau_ref[h:h + 1, :], jnp.exp(score - lse_ref[h:h + 1, :]), 0.0)
        act = _gelu_tanh(ht[cc * PEER_N_KEYS:(cc + 1) * PEER_N_KEYS, :]) * gate
        a_ref[cc * PEER_N_KEYS:(cc + 1) * PEER_N_KEYS, :] = act.astype(BF16)
    acc_ref[...] += _dot(vt_ref[...], a_ref[...])

    @pl.when(e == pl.num_programs(1) - 1)
    def _():
        y = _layer_norm_rows(DEEPNORM_ALPHA * x_ref[...] + acc_ref[...].T, g_ref[...], b_ref[...])
        o_ref[...] = y
        ob_ref[...] = y.astype(BF16)


def peer_main(xb, x, u, vt, s1, s2, tau, lse, g, b, *, tm, te):
    m, d = x.shape
    ne = u.shape[0]
    nh = PEER_HEADS
    kernel = functools.partial(_peer_main_kernel, te=te)
    return pl.pallas_call(
        kernel,
        grid=(m // tm, ne // te),
        in_specs=[pl.BlockSpec((tm, d), lambda i, e: (i, 0)),
                  pl.BlockSpec((te, d), lambda i, e: (e, 0)),
                  pl.BlockSpec((d, te), lambda i, e: (0, e)),
                  pl.BlockSpec((nh, PEER_N_KEYS, tm), lambda i, e: (0, 0, i)),
                  pl.BlockSpec((nh, PEER_N_KEYS, tm), lambda i, e: (0, 0, i)),
                  pl.BlockSpec((nh, tm), lambda i, e: (0, i)),
                  pl.BlockSpec((nh, tm), lambda i, e: (0, i)),
                  pl.BlockSpec((tm, d), lambda i, e: (i, 0)),
                  pl.BlockSpec((1, d), lambda i, e: (0, 0)),
                  pl.BlockSpec((1, d), lambda i, e: (0, 0))],
        out_specs=[pl.BlockSpec((tm, d), lambda i, e: (i, 0)),
                   pl.BlockSpec((tm, d), lambda i, e: (i, 0))],
        out_shape=[jax.ShapeDtypeStruct((m, d), F32), jax.ShapeDtypeStruct((m, d), BF16)],
        scratch_shapes=[pltpu.VMEM((d, tm), F32), pltpu.VMEM((te, tm), BF16)],
        compiler_params=_cparams("parallel", "arbitrary"),
        name="peer_main",
    )(xb, u, vt, s1, s2, tau, lse, x, g.reshape(1, d), b.reshape(1, d))


TOKEN_TILE = 1024
PROJ_COL_TILE = 512
OUT_K_TILE = 512
OUT_ROW_TILE = 512
PEER_TOKEN_TILE = 512
PEER_EXPERT_TILE = 512


def kernel(x_prompt, x_sample, state_gdn_conv, state_gdn_s, state_mlstm_c, state_mlstm_n, state_mlstm_m,
           gdn_w_in, gdn_conv_w, gdn_a_log, gdn_dt_bias, gdn_norm_w, gdn_w_out,
           ml_w_in, ml_gate_b, ml_norm_w, ml_w_out,
           ln_mix_g, ln_mix_b, ln_ffn_g, ln_ffn_b,
           peer_w_q, peer_keys, peer_u, peer_v):
    bp, tp, d = x_prompt.shape
    bs, ts, _ = x_sample.shape
    np_tok = bp * tp
    ns_tok = bs * ts
    x = jnp.concatenate([x_prompt.reshape(np_tok, d), x_sample.reshape(ns_tok, d)], axis=0)
    xb = x.astype(BF16)
    cp = min(GDN_CHUNK, tp)
    cs = min(GDN_CHUNK, ts)

    p_conv, p_s, p_c, p_n, p_m = [], [], [], [], []
    s_conv, s_s, s_c, s_n, s_m = [], [], [], [], []
    for layer in range(DEPTH):
        j = layer // 2
        if layer % 2 == 0:
            w_in = gdn_w_in[j]
            proj = matmul(xb, w_in, GDN_MAIN_DIM, tm=TOKEN_TILE, tn=PROJ_COL_TILE)
            gates, gates_t = gate_proj(xb, w_in[:, GDN_MAIN_DIM:], tm=TOKEN_TILE)
            shared = (gdn_conv_w[j], gdn_a_log[j], gdn_dt_bias[j], gdn_norm_w[j])
            zc = jnp.zeros((bp,) + state_gdn_conv.shape[2:], F32)
            zs = jnp.zeros((bp,) + state_gdn_s.shape[2:], F32)
            o_p, conv_p, st_p = gdn_group(proj, gates, gates_t, zc, zs, *shared, row0=0, nb=bp, t=tp, c=cp)
            o_s, conv_s, st_s = gdn_group(proj, gates, gates_t, state_gdn_conv[j], state_gdn_s[j], *shared,
                                          row0=np_tok, nb=bs, t=ts, c=cs)
            p_conv.append(conv_p), p_s.append(st_p), s_conv.append(conv_s), s_s.append(st_s)
            mix_in = jnp.concatenate([o_p, o_s], axis=0)
            w_out = gdn_w_out[j]
        else:
            w_in = ml_w_in[j]
            proj = matmul(xb, w_in, ML_MAIN_DIM, tm=TOKEN_TILE, tn=PROJ_COL_TILE)
            gates, gates_t = gate_proj(xb, w_in[:, ML_MAIN_DIM:], tm=TOKEN_TILE)
            shared = (ml_gate_b[j], ml_norm_w[j])
            zc = jnp.zeros((bp,) + state_mlstm_c.shape[2:], F32)
            zn = jnp.zeros((bp,) + state_mlstm_n.shape[2:], F32)
            zm = jnp.zeros((bp,) + state_mlstm_m.shape[2:], F32)
            o_p, c_p, n_p, m_p = mlstm_group(proj, gates, gates_t, zc, zn, zm, *shared, row0=0, nb=bp, t=tp, c=cp)
            o_s, c_s, n_s, m_s = mlstm_group(proj, gates, gates_t, state_mlstm_c[j], state_mlstm_n[j],
                                             state_mlstm_m[j], *shared, row0=np_tok, nb=bs, t=ts, c=cs)
            p_c.append(c_p), p_n.append(n_p), p_m.append(m_p), s_c.append(c_s), s_n.append(n_s), s_m.append(m_s)
            mix_in = jnp.concatenate([o_p, o_s], axis=0)
            w_out = ml_w_out[j]
        x, xb = out_proj_ln(mix_in, w_out, x, ln_mix_g[layer], ln_mix_b[layer], tm=OUT_ROW_TILE, tk=OUT_K_TILE)

        q = matmul(xb, peer_w_q[layer], peer_w_q.shape[2], tm=TOKEN_TILE, tn=PROJ_COL_TILE)
        s1, s2, tau, lse = peer_route(q, peer_keys[layer], tm=PEER_TOKEN_TILE)
        x, xb = peer_main(xb, x, peer_u[layer].astype(BF16), peer_v[layer].astype(BF16).T, s1, s2, tau, lse,
                          ln_ffn_g[layer], ln_ffn_b[layer], tm=PEER_TOKEN_TILE, te=PEER_EXPERT_TILE)

    y_prompt = x[:np_tok].reshape(bp, tp, d)
    y_sample = x[np_tok:].reshape(bs, ts, d)
    return (y_prompt, y_sample,
            jnp.stack(p_conv), jnp.stack(p_s), jnp.stack(p_c), jnp.stack(p_n), jnp.stack(p_m),
            jnp.stack(s_conv), jnp.stack(s_s), jnp.stack(s_c), jnp.stack(s_n), jnp.stack(s_m))
```

```python
import functools

import jax
import jax.numpy as jnp
from jax import lax
from jax.experimental import pallas as pl
from jax.experimental.pallas import tpu as pltpu

F32 = jnp.float32
BF16 = jnp.bfloat16

D_MODEL = 2048
DEPTH = 4
GDN_K_HEADS = 16
GDN_V_HEADS = 32
GDN_DK = 128
GDN_DV = 128
GDN_QK_DIM = GDN_K_HEADS * GDN_DK
GDN_V_DIM = GDN_V_HEADS * GDN_DV
GDN_CONV_DIM = 2 * GDN_QK_DIM + GDN_V_DIM
GDN_MAIN_DIM = GDN_CONV_DIM + GDN_V_DIM
GDN_CONV_W = 4
GDN_CHUNK = 64
ML_HEADS = 8
ML_DQK = 128
ML_DV = 256
ML_QK_DIM = ML_HEADS * ML_DQK
ML_V_DIM = ML_HEADS * ML_DV
ML_MAIN_DIM = 2 * ML_QK_DIM + 2 * ML_V_DIM
ML_CHUNK = 64
PEER_HEADS = 8
PEER_N_KEYS = 128
PEER_HALF = 128
PEER_TOPK = 16
DEEPNORM_ALPHA = (2 * DEPTH) ** 0.25
LN_EPS = 1e-5
RMS_EPS = 1e-6
NEG_BIG = -1e30
NEG_INF = float("-inf")
POS_INF = float("inf")

VMEM_LIMIT_BYTES = 56 * 1024 * 1024
HIGHEST = lax.Precision.HIGHEST


def _cparams(*sem):
    return pltpu.CompilerParams(dimension_semantics=sem, vmem_limit_bytes=VMEM_LIMIT_BYTES)


def _dot(a, b):
    return jnp.dot(a.astype(BF16), b.astype(BF16), preferred_element_type=F32)


def _dot_nt(a, b):
    return lax.dot_general(a.astype(BF16), b.astype(BF16), (((1,), (1,)), ((), ())),
                           preferred_element_type=F32)


def _dot_tn(a, b):
    return lax.dot_general(a.astype(BF16), b.astype(BF16), (((0,), (0,)), ((), ())),
                           preferred_element_type=F32)


def _dot_f32(a, b):
    return jnp.dot(a, b, preferred_element_type=F32, precision=HIGHEST)


def _sigmoid(x):
    return 1.0 / (1.0 + jnp.exp(-x))


def _silu(x):
    return x * _sigmoid(x)


def _softplus(x):
    return jnp.maximum(x, 0.0) + jnp.log(1.0 + jnp.exp(-jnp.abs(x)))


def _log_sigmoid(x):
    return -_softplus(-x)


def _gelu_tanh(x):
    return 0.5 * x * (1.0 + jnp.tanh(0.7978845608028654 * (x + 0.044715 * (x * x * x))))


def _lower(c, strict):
    r = lax.broadcasted_iota(jnp.int32, (c, c), 0)
    k = lax.broadcasted_iota(jnp.int32, (c, c), 1)
    return (r > k) if strict else (r >= k)


def _upper(c):
    r = lax.broadcasted_iota(jnp.int32, (c, c), 0)
    k = lax.broadcasted_iota(jnp.int32, (c, c), 1)
    return r <= k


def _layer_norm_rows(v, g, b):
    mu = jnp.mean(v, axis=-1, keepdims=True)
    d = v - mu
    var = jnp.mean(d * d, axis=-1, keepdims=True)
    return d * lax.rsqrt(var + LN_EPS) * g + b


def _matmul_kernel(x_ref, w_ref, o_ref):
    o_ref[...] = _dot(x_ref[...], w_ref[...])


def matmul(x, w, n_cols, *, tm, tn):
    m, k = x.shape
    return pl.pallas_call(
        _matmul_kernel,
        grid=(m // tm, n_cols // tn),
        in_specs=[pl.BlockSpec((tm, k), lambda i, j: (i, 0)),
                  pl.BlockSpec((k, tn), lambda i, j: (0, j))],
        out_specs=pl.BlockSpec((tm, tn), lambda i, j: (i, j)),
        out_shape=jax.ShapeDtypeStruct((m, n_cols), F32),
        compiler_params=_cparams("parallel", "arbitrary"),
        name="proj_matmul",
    )(x, w)


def _gate_proj_kernel(x_ref, w_ref, wt_ref, o_ref, ot_ref):
    x = x_ref[...]
    o_ref[...] = _dot(x, w_ref[...])
    ot_ref[...] = _dot_nt(wt_ref[...], x)


def gate_proj(x, w_gate, *, tm):
    m, k = x.shape
    n = w_gate.shape[1]
    return pl.pallas_call(
        _gate_proj_kernel,
        grid=(m // tm,),
        in_specs=[pl.BlockSpec((tm, k), lambda i: (i, 0)),
                  pl.BlockSpec((k, n), lambda i: (0, 0)),
                  pl.BlockSpec((n, k), lambda i: (0, 0))],
        out_specs=[pl.BlockSpec((tm, n), lambda i: (i, 0)),
                   pl.BlockSpec((n, tm), lambda i: (0, i))],
        out_shape=[jax.ShapeDtypeStruct((m, n), F32), jax.ShapeDtypeStruct((n, m), F32)],
        compiler_params=_cparams("parallel"),
        name="gate_proj",
    )(x, w_gate, w_gate.T)


def _out_proj_ln_kernel(h_ref, w_ref, x_ref, g_ref, b_ref, o_ref, ob_ref, acc_ref):
    kk = pl.program_id(1)

    @pl.when(kk == 0)
    def _():
        acc_ref[...] = jnp.zeros_like(acc_ref)

    acc_ref[...] += _dot(h_ref[...], w_ref[...])

    @pl.when(kk == pl.num_programs(1) - 1)
    def _():
        y = _layer_norm_rows(DEEPNORM_ALPHA * x_ref[...] + acc_ref[...], g_ref[...], b_ref[...])
        o_ref[...] = y
        ob_ref[...] = y.astype(BF16)


def out_proj_ln(h, w, x, g, b, *, tm, tk):
    m, k = h.shape
    d = w.shape[1]
    return pl.pallas_call(
        _out_proj_ln_kernel,
        grid=(m // tm, k // tk),
        in_specs=[pl.BlockSpec((tm, tk), lambda i, j: (i, j)),
                  pl.BlockSpec((tk, d), lambda i, j: (j, 0)),
                  pl.BlockSpec((tm, d), lambda i, j: (i, 0)),
                  pl.BlockSpec((1, d), lambda i, j: (0, 0)),
                  pl.BlockSpec((1, d), lambda i, j: (0, 0))],
        out_specs=[pl.BlockSpec((tm, d), lambda i, j: (i, 0)),
                   pl.BlockSpec((tm, d), lambda i, j: (i, 0))],
        out_shape=[jax.ShapeDtypeStruct((m, d), F32), jax.ShapeDtypeStruct((m, d), BF16)],
        scratch_shapes=[pltpu.VMEM((tm, d), F32)],
        compiler_params=_cparams("parallel", "arbitrary"),
        name="out_proj_ln",
    )(h, w, x, g.reshape(1, d), b.reshape(1, d))


CONV_PAD = 8


def _dot_split(e, r):
    c = e.shape[0]
    e_hi = e.astype(BF16).astype(F32)
    r_hi = r.astype(BF16).astype(F32)
    stacked = _dot(jnp.concatenate([e_hi, e - e_hi], axis=0), r_hi)
    return stacked[0:c] + stacked[c:2 * c] + _dot(e_hi, r - r_hi)


def _gdn_kernel(qkv_ref, z_ref, bpre_ref, apre_ref, bpre_t_ref, apre_t_ref, conv0_ref, s0_ref,
                convw_ref, alog_ref, dtb_ref, alog_t_ref, dtb_t_ref, normw_ref,
                o_ref, convo_ref, so_ref,
                xp_ref, s_ref, kq_ref, l_ref, a_ref, e_ref, p_ref, rhs_ref, lhs_ref, kd_ref, vn_ref, *, c):
    step = pl.program_id(1)
    hist = GDN_CONV_W - 1
    heads = range(GDN_V_HEADS)
    rep = GDN_V_HEADS // GDN_K_HEADS
    tok = slice(CONV_PAD, CONV_PAD + c)

    @pl.when(step == 0)
    def _():
        xp_ref[CONV_PAD - hist:CONV_PAD, :] = conv0_ref[0]
        s_ref[...] = s0_ref[0]

    x = qkv_ref[0]
    xp_ref[tok, :] = x
    y = xp_ref[CONV_PAD - hist:CONV_PAD - hist + c, :] * convw_ref[0:1, :]
    for i in range(1, GDN_CONV_W):
        y = y + xp_ref[CONV_PAD - hist + i:CONV_PAD - hist + i + c, :] * convw_ref[i:i + 1, :]
    tail = x[c - hist:c, :]
    xp_ref[CONV_PAD - hist:CONV_PAD, :] = tail
    convo_ref[0] = tail
    xp_ref[tok, :] = _silu(y)

    beta_c = _sigmoid(bpre_ref[0])
    g_c = -jnp.exp(alog_ref[...]) * _softplus(apre_ref[0] + dtb_ref[...])
    g_r = -jnp.exp(alog_t_ref[...]) * _softplus(apre_t_ref[0] + dtb_t_ref[...])
    rows = lax.broadcasted_iota(jnp.int32, (c, c), 0)
    cols = lax.broadcasted_iota(jnp.int32, (c, c), 1)
    causal = rows >= cols
    strict = rows > cols
    gc_c = _dot_f32(causal.astype(F32), g_c)
    gc_r = _dot_f32(g_r, (rows <= cols).astype(F32))

    for kh in range(GDN_K_HEADS):
        qs = slice(kh * GDN_DK, (kh + 1) * GDN_DK)
        ks = slice(GDN_QK_DIM + kh * GDN_DK, GDN_QK_DIM + (kh + 1) * GDN_DK)
        q = xp_ref[tok, qs]
        k = xp_ref[tok, ks]
        q = q * lax.rsqrt(jnp.sum(q * q, axis=-1, keepdims=True) + RMS_EPS) * (GDN_DK ** -0.5)
        k = k * lax.rsqrt(jnp.sum(k * k, axis=-1, keepdims=True) + RMS_EPS)
        xp_ref[tok, qs] = q
        xp_ref[tok, ks] = k
        kq_ref[kh] = _dot_nt(jnp.concatenate([k, q], axis=0), k)

    diag8 = strict & ((rows >> 3) == (cols >> 3))
    for h in heads:
        kh = h // rep
        q = xp_ref[tok, kh * GDN_DK:(kh + 1) * GDN_DK]
        k = xp_ref[tok, GDN_QK_DIM + kh * GDN_DK:GDN_QK_DIM + (kh + 1) * GDN_DK]
        v = xp_ref[tok, 2 * GDN_QK_DIM + h * GDN_DV:2 * GDN_QK_DIM + (h + 1) * GDN_DV]
        gcol = gc_c[:, h:h + 1]
        bcol = beta_c[:, h:h + 1]
        diff = gcol - gc_r[h:h + 1, :]
        egc = jnp.exp(gcol)
        l = bcol * kq_ref[kh, 0:c, :] * jnp.exp(jnp.where(strict, diff, NEG_BIG))
        l_ref[h] = l
        a_ref[h] = kq_ref[kh, c:2 * c, :] * jnp.exp(jnp.where(causal, diff, NEG_BIG))
        rhs_ref[h, :, 0:GDN_DV] = v * bcol
        rhs_ref[h, :, GDN_DV:GDN_DV + GDN_DK] = k * (bcol * egc)
        lhs_ref[h, c:2 * c, :] = q * egc
        kd_ref[h] = k * jnp.exp(gcol[c - 1:c, :] - gcol)
        l8 = jnp.where(diag8, l, 0.0)
        e_ref[h] = -l8
        p_ref[h] = _dot(l8, l8)
    for h in heads:
        e = e_ref[h]
        p = p_ref[h]
        e_ref[h] = e + p + _dot(e, p)
        p_ref[h] = _dot(p, p)
    for h in heads:
        e = e_ref[h]
        p = p_ref[h]
        e_ref[h] = e + p + _dot(e, p)
    shift = 3
    while (2 << shift) <= c:
        lower_left = (((rows >> (shift + 1)) == (cols >> (shift + 1)))
                      & (((rows >> shift) & 1) == 1) & (((cols >> shift) & 1) == 0))
        for h in heads:
            n = jnp.where(lower_left, l_ref[h], 0.0)
            p_ref[h] = n + _dot(e_ref[h], n)
        for h in heads:
            t = p_ref[h]
            e = e_ref[h]
            e_ref[h] = e - t - _dot(t, e)
        shift += 1

    for h in heads:
        r = rhs_ref[h]
        uw = r + _dot_split(e_ref[h], r)
        rhs_ref[h, :, 0:GDN_DV] = uw[:, 0:GDN_DV]
        lhs_ref[h, 0:c, :] = uw[:, GDN_DV:GDN_DV + GDN_DK]
    for h in heads:
        ws = _dot(lhs_ref[h], s_ref[h])
        vn_ref[h] = rhs_ref[h, :, 0:GDN_DV] - ws[0:c]
        lhs_ref[h, c:2 * c, :] = ws[c:2 * c]
    for h in heads:
        v_new = vn_ref[h]
        o = lhs_ref[h, c:2 * c, :] + _dot(a_ref[h], v_new)
        s_ref[h] = s_ref[h] * jnp.exp(gc_c[c - 1:c, h:h + 1]) + _dot_tn(kd_ref[h], v_new)
        o = o * lax.rsqrt(jnp.mean(o * o, axis=-1, keepdims=True) + RMS_EPS) * normw_ref[...]
        zh = z_ref[0, :, h * GDN_DV:(h + 1) * GDN_DV]
        o_ref[0, :, h * GDN_DV:(h + 1) * GDN_DV] = (o * _silu(zh)).astype(o_ref.dtype)

    @pl.when(step == pl.num_programs(1) - 1)
    def _():
        so_ref[0] = s_ref[...]


def gdn_group(proj, gates, gates_t, conv0, s0, conv_w, a_log, dt_bias, norm_w, *, row0, nb, t, c):
    tokens = proj.shape[0]
    nchunk = t // c
    blk0 = row0 // c
    hv = GDN_V_HEADS
    proj3 = proj.reshape(tokens // c, c, proj.shape[1])
    bpre = gates[:, :hv].reshape(tokens // c, c, hv)
    apre = gates[:, hv:].reshape(tokens // c, c, hv)
    bpre_t = gates_t[:hv].reshape(hv, tokens // c, c).transpose(1, 0, 2)
    apre_t = gates_t[hv:].reshape(hv, tokens // c, c).transpose(1, 0, 2)
    z_blk = GDN_CONV_DIM // GDN_V_DIM

    def rows(b, s):
        return blk0 + b * nchunk + s

    kernel = functools.partial(_gdn_kernel, c=c)
    o, conv, s = pl.pallas_call(
        kernel,
        grid=(nb, nchunk),
        in_specs=[
            pl.BlockSpec((1, c, GDN_CONV_DIM), lambda b, s: (rows(b, s), 0, 0)),
            pl.BlockSpec((1, c, GDN_V_DIM), lambda b, s: (rows(b, s), 0, z_blk)),
            pl.BlockSpec((1, c, hv), lambda b, s: (rows(b, s), 0, 0)),
            pl.BlockSpec((1, c, hv), lambda b, s: (rows(b, s), 0, 0)),
            pl.BlockSpec((1, hv, c), lambda b, s: (rows(b, s), 0, 0)),
            pl.BlockSpec((1, hv, c), lambda b, s: (rows(b, s), 0, 0)),
            pl.BlockSpec((1, GDN_CONV_W - 1, GDN_CONV_DIM), lambda b, s: (b, 0, 0)),
            pl.BlockSpec((1, hv, GDN_DK, GDN_DV), lambda b, s: (b, 0, 0, 0)),
            pl.BlockSpec((GDN_CONV_W, GDN_CONV_DIM), lambda b, s: (0, 0)),
            pl.BlockSpec((1, hv), lambda b, s: (0, 0)),
            pl.BlockSpec((1, hv), lambda b, s: (0, 0)),
            pl.BlockSpec((hv, 1), lambda b, s: (0, 0)),
            pl.BlockSpec((hv, 1), lambda b, s: (0, 0)),
            pl.BlockSpec((1, GDN_DV), lambda b, s: (0, 0)),
        ],
        out_specs=[
            pl.BlockSpec((1, c, GDN_V_DIM), lambda b, s: (b * nchunk + s, 0, 0)),
            pl.BlockSpec((1, GDN_CONV_W - 1, GDN_CONV_DIM), lambda b, s: (b, 0, 0)),
            pl.BlockSpec((1, hv, GDN_DK, GDN_DV), lambda b, s: (b, 0, 0, 0)),
        ],
        out_shape=[
            jax.ShapeDtypeStruct((nb * nchunk, c, GDN_V_DIM), BF16),
            jax.ShapeDtypeStruct((nb, GDN_CONV_W - 1, GDN_CONV_DIM), F32),
            jax.ShapeDtypeStruct((nb, hv, GDN_DK, GDN_DV), F32),
        ],
        scratch_shapes=[pltpu.VMEM((CONV_PAD + c, GDN_CONV_DIM), F32),
                        pltpu.VMEM((hv, GDN_DK, GDN_DV), F32),
                        pltpu.VMEM((GDN_K_HEADS, 2 * c, c), F32),
                        pltpu.VMEM((hv, c, c), F32),
                        pltpu.VMEM((hv, c, c), F32),
                        pltpu.VMEM((hv, c, c), F32),
                        pltpu.VMEM((hv, c, c), F32),
                        pltpu.VMEM((hv, c, GDN_DV + GDN_DK), F32),
                        pltpu.VMEM((hv, 2 * c, GDN_DK), F32),
                        pltpu.VMEM((hv, c, GDN_DK), F32),
                        pltpu.VMEM((hv, c, GDN_DV), F32)],
        compiler_params=_cparams("parallel", "arbitrary"),
        name=f"gdn_chunk{c}",
    )(proj3, proj3, bpre, apre, bpre_t, apre_t, conv0, s0, conv_w,
      a_log.reshape(1, hv), dt_bias.reshape(1, hv), a_log.reshape(hv, 1), dt_bias.reshape(hv, 1),
      norm_w.reshape(1, GDN_DV))
    return o.reshape(nb * t, GDN_V_DIM), conv, s


def _mlstm_kernel(main_ref, ig_ref, fg_ref, ig_t_ref, fg_t_ref, c0_ref, n0_ref, m0_ref,
                  bi_ref, bf_ref, bi_t_ref, bf_t_ref, normw_ref,
                  o_ref, co_ref, no_ref, mo_ref,
                  c_ref, n_ref, m_ref, *, c):
    step = pl.program_id(1)

    @pl.when(step == 0)
    def _():
        c_ref[...] = c0_ref[0]
        n_ref[...] = n0_ref[0]
        m_ref[...] = jnp.broadcast_to(m0_ref[0], m_ref.shape)

    ig_c = ig_ref[0] + bi_ref[...]
    lf_c = _log_sigmoid(fg_ref[0] + bf_ref[...])
    ig_r = ig_t_ref[0] + bi_t_ref[...]
    lf_r = _log_sigmoid(fg_t_ref[0] + bf_t_ref[...])
    causal = _lower(c, strict=False)
    bc_c = _dot_f32(causal.astype(F32), lf_c)
    bc_r = _dot_f32(lf_r, _upper(c).astype(F32))

    for h in range(ML_HEADS):
        q = main_ref[0, :, h * ML_DQK:(h + 1) * ML_DQK]
        k = main_ref[0, :, ML_QK_DIM + h * ML_DQK:ML_QK_DIM + (h + 1) * ML_DQK] * (ML_DQK ** -0.5)
        v = main_ref[0, :, 2 * ML_QK_DIM + h * ML_DV:2 * ML_QK_DIM + (h + 1) * ML_DV]
        o_pre = main_ref[0, :, 2 * ML_QK_DIM + ML_V_DIM + h * ML_DV:2 * ML_QK_DIM + ML_V_DIM + (h + 1) * ML_DV]
        bcol = bc_c[:, h:h + 1]
        brow = bc_r[h:h + 1, :]
        icol = ig_c[:, h:h + 1]
        irow = ig_r[h:h + 1, :]
        m_prev = m_ref[h:h + 1, 0:1]
        cm = c_ref[h]
        nv = n_ref[h:h + 1, :]

        d = jnp.where(causal, bcol - brow + irow, NEG_BIG)
        inter = bcol + m_prev
        m_t = jnp.maximum(inter, jnp.max(d, axis=-1, keepdims=True))
        w_intra = jnp.exp(d - m_t)
        w_inter = jnp.exp(inter - m_t)
        sqk = _dot_nt(q, k) * w_intra
        num = w_inter * _dot(q, cm) + _dot(sqk, v)
        den = w_inter * jnp.sum(q * nv, axis=-1, keepdims=True) + jnp.sum(sqk, axis=-1, keepdims=True)
        hid = num / jnp.maximum(jnp.abs(den), jnp.exp(-m_t))

        b_last = bcol[c - 1:c, :]
        d_end = b_last - bcol + icol
        m_new = jnp.maximum(b_last + m_prev, jnp.max(d_end, axis=0, keepdims=True))
        wk = jnp.exp(d_end - m_new) * k
        scale = jnp.exp(b_last + m_prev - m_new)
        c_ref[h] = scale * cm + _dot_tn(wk, v)
        n_ref[h:h + 1, :] = scale * nv + jnp.sum(wk, axis=0, keepdims=True)
        m_ref[h:h + 1, :] = jnp.broadcast_to(m_new, (1, m_ref.shape[1]))

        hid = hid * lax.rsqrt(jnp.mean(hid * hid, axis=-1, keepdims=True) + RMS_EPS) * normw_ref[...]
        o_ref[0, :, h * ML_DV:(h + 1) * ML_DV] = (hid * _sigmoid(o_pre)).astype(o_ref.dtype)

    @pl.when(step == pl.num_programs(1) - 1)
    def _():
        co_ref[0] = c_ref[...]
        no_ref[0] = n_ref[...]
        mo_ref[0] = m_ref[:, 0:1]


def mlstm_group(proj, gates, gates_t, c0, n0, m0, gate_b, norm_w, *, row0, nb, t, c):
    tokens = proj.shape[0]
    nchunk = t // c
    blk0 = row0 // c
    nh = ML_HEADS
    proj3 = proj.reshape(tokens // c, c, proj.shape[1])
    ig = gates[:, :nh].reshape(tokens // c, c, nh)
    fg = gates[:, nh:].reshape(tokens // c, c, nh)
    ig_t = gates_t[:nh].reshape(nh, tokens // c, c).transpose(1, 0, 2)
    fg_t = gates_t[nh:].reshape(nh, tokens // c, c).transpose(1, 0, 2)

    def rows(b, s):
        return blk0 + b * nchunk + s

    kernel = functools.partial(_mlstm_kernel, c=c)
    o, cm, nv, m = pl.pallas_call(
        kernel,
        grid=(nb, nchunk),
        in_specs=[
            pl.BlockSpec((1, c, ML_MAIN_DIM), lambda b, s: (rows(b, s), 0, 0)),
            pl.BlockSpec((1, c, nh), lambda b, s: (rows(b, s), 0, 0)),
            pl.BlockSpec((1, c, nh), lambda b, s: (rows(b, s), 0, 0)),
            pl.BlockSpec((1, nh, c), lambda b, s: (rows(b, s), 0, 0)),
            pl.BlockSpec((1, nh, c), lambda b, s: (rows(b, s), 0, 0)),
            pl.BlockSpec((1, nh, ML_DQK, ML_DV), lambda b, s: (b, 0, 0, 0)),
            pl.BlockSpec((1, nh, ML_DQK), lambda b, s: (b, 0, 0)),
            pl.BlockSpec((1, nh, 1), lambda b, s: (b, 0, 0)),
            pl.BlockSpec((1, nh), lambda b, s: (0, 0)),
            pl.BlockSpec((1, nh), lambda b, s: (0, 0)),
            pl.BlockSpec((nh, 1), lambda b, s: (0, 0)),
            pl.BlockSpec((nh, 1), lambda b, s: (0, 0)),
            pl.BlockSpec((1, ML_DV), lambda b, s: (0, 0)),
        ],
        out_specs=[
            pl.BlockSpec((1, c, ML_V_DIM), lambda b, s: (b * nchunk + s, 0, 0)),
            pl.BlockSpec((1, nh, ML_DQK, ML_DV), lambda b, s: (b, 0, 0, 0)),
            pl.BlockSpec((1, nh, ML_DQK), lambda b, s: (b, 0, 0)),
            pl.BlockSpec((1, nh, 1), lambda b, s: (b, 0, 0)),
        ],
        out_shape=[
            jax.ShapeDtypeStruct((nb * nchunk, c, ML_V_DIM), BF16),
            jax.ShapeDtypeStruct((nb, nh, ML_DQK, ML_DV), F32),
            jax.ShapeDtypeStruct((nb, nh, ML_DQK), F32),
            jax.ShapeDtypeStruct((nb, nh, 1), F32),
        ],
        scratch_shapes=[pltpu.VMEM((nh, ML_DQK, ML_DV), F32),
                        pltpu.VMEM((nh, ML_DQK), F32),
                        pltpu.VMEM((nh, 128), F32)],
        compiler_params=_cparams("parallel", "arbitrary"),
        name=f"mlstm_chunk{c}",
    )(proj3, ig, fg, ig_t, fg_t, c0, n0, m0.reshape(nb, nh, 1),
      gate_b[:nh].reshape(1, nh), gate_b[nh:].reshape(1, nh),
      gate_b[:nh].reshape(nh, 1), gate_b[nh:].reshape(nh, 1), norm_w.reshape(1, ML_DV))
    return o.reshape(nb * t, ML_V_DIM), cm, nv, m.reshape(nb, nh)


_CAND_PAIRS = [(a, b) for a in range(PEER_TOPK) for b in range(PEER_TOPK) if (a + 1) * (b + 1) <= PEER_TOPK]
_CAND_ROWS = -(-len(_CAND_PAIRS) // 8) * 8


def _top_values(work, count):
    rows = lax.broadcasted_iota(jnp.int32, work.shape, 0)
    out = []
    for r in range(count):
        m = jnp.max(work, axis=0, keepdims=True)
        out.append(m)
        if r + 1 < count:
            first = jnp.min(jnp.where(work == m, rows, work.shape[0]), axis=0, keepdims=True)
            work = jnp.where(rows == first, NEG_INF, work)
    return out


def _peer_route_kernel(q_ref, keys_ref, th1_ref, w1_ref, s2_ref, w2_ref, cand_ref):
    cand_ref[...] = jnp.full(cand_ref.shape, NEG_INF, F32)
    for h in range(PEER_HEADS):
        scores, tops = [], []
        for p in range(2):
            col = (2 * h + p) * PEER_HALF
            s = _dot_nt(keys_ref[h, p], q_ref[:, col:col + PEER_HALF])
            scores.append(s)
            tops.append(_top_values(s, PEER_TOPK))
        for i, (a, b) in enumerate(_CAND_PAIRS):
            cand_ref[i:i + 1, :] = tops[0][a] + tops[1][b]
        cand = cand_ref[...]
        tau = _top_values(cand, PEER_TOPK)[-1]
        max1, max2 = tops[0][0], tops[1][0]
        z = jnp.sum(jnp.where(cand >= tau, jnp.exp(cand - (max1 + max2)), 0.0), axis=0, keepdims=True)
        cut = jnp.full(scores[0].shape, POS_INF, F32)
        for a in range(PEER_TOPK):
            cut_a = jnp.full(tau.shape, POS_INF, F32)
            for b in range(PEER_TOPK // (a + 1)):
                cut_a = jnp.where(tops[0][a] + tops[1][b] >= tau, tops[1][b], cut_a)
            cut = jnp.where(scores[0] == tops[0][a], cut_a, cut)
        th1_ref[h] = cut
        w1_ref[h] = jnp.exp(scores[0] - max1)
        s2_ref[h] = scores[1]
        w2_ref[h] = jnp.exp(scores[1] - max2) / z


def peer_route(q, keys, *, tm):
    m = q.shape[0]
    nh = PEER_HEADS
    table = jax.ShapeDtypeStruct((nh, PEER_N_KEYS, m), F32)
    table_spec = pl.BlockSpec((nh, PEER_N_KEYS, tm), lambda i: (0, 0, i))
    return pl.pallas_call(
        _peer_route_kernel,
        grid=(m // tm,),
        in_specs=[pl.BlockSpec((tm, q.shape[1]), lambda i: (i, 0)),
                  pl.BlockSpec(keys.shape, lambda i: (0, 0, 0, 0))],
        out_specs=[table_spec] * 4,
        out_shape=[table] * 4,
        scratch_shapes=[pltpu.VMEM((_CAND_ROWS, tm), F32)],
        compiler_params=_cparams("parallel"),
        name="peer_route",
    )(q, keys)


PEER_LANE_TILE = 256


def _peer_main_kernel(xb_ref, u_ref, vt_ref, th1_ref, w1_ref, s2_ref, w2_ref, x_ref, g_ref, b_ref,
                      o_ref, ob_ref, acc_ref, ht_ref, a_ref, *, te):
    e = pl.program_id(1)
    groups = te // PEER_N_KEYS
    tm = ht_ref.shape[1]

    @pl.when(e == 0)
    def _():
        acc_ref[...] = jnp.zeros_like(acc_ref)

    ht_ref[...] = _dot_nt(u_ref[...], xb_ref[...])
    for cc in range(groups):
        key1 = e * groups + cc
        rows = slice(cc * PEER_N_KEYS, (cc + 1) * PEER_N_KEYS)
        for t0 in range(0, tm, PEER_LANE_TILE):
            lanes = slice(t0, t0 + PEER_LANE_TILE)
            gate = jnp.zeros((PEER_N_KEYS, PEER_LANE_TILE), F32)
            for h in range(PEER_HEADS):
                th = th1_ref[h, pl.ds(key1, 1), lanes]
                w1 = w1_ref[h, pl.ds(key1, 1), lanes]
                gate = gate + jnp.where(s2_ref[h, :, lanes] >= th, w2_ref[h, :, lanes], 0.0) * w1
            a_ref[rows, lanes] = (_gelu_tanh(ht_ref[rows, lanes]) * gate).astype(BF16)
    acc_ref[...] += _dot(vt_ref[...], a_ref[...])

    @pl.when(e == pl.num_programs(1) - 1)
    def _():
        y = _layer_norm_rows(DEEPNORM_ALPHA * x_ref[...] + acc_ref[...].T, g_ref[...], b_ref[...])
        o_ref[...] = y
        ob_ref[...] = y.astype(BF16)


def peer_main(xb, x, u, vt, th1, w1, s2, w2, g, b, *, tm, te):
    m, d = x.shape
    ne = u.shape[0]
    nh = PEER_HEADS
    kernel = functools.partial(_peer_main_kernel, te=te)
    once = pl.Buffered(1)
    table_spec = pl.BlockSpec((nh, PEER_N_KEYS, tm), lambda i, e: (0, 0, i), pipeline_mode=once)
    return pl.pallas_call(
        kernel,
        grid=(m // tm, ne // te),
        in_specs=[pl.BlockSpec((tm, d), lambda i, e: (i, 0), pipeline_mode=once),
                  pl.BlockSpec((te, d), lambda i, e: (e, 0)),
                  pl.BlockSpec((d, te), lambda i, e: (0, e)),
                  table_spec, table_spec, table_spec, table_spec,
                  pl.BlockSpec((tm, d), lambda i, e: (i, 0), pipeline_mode=once),
                  pl.BlockSpec((1, d), lambda i, e: (0, 0)),
                  pl.BlockSpec((1, d), lambda i, e: (0, 0))],
        out_specs=[pl.BlockSpec((tm, d), lambda i, e: (i, 0)),
                   pl.BlockSpec((tm, d), lambda i, e: (i, 0))],
        out_shape=[jax.ShapeDtypeStruct((m, d), F32), jax.ShapeDtypeStruct((m, d), BF16)],
        scratch_shapes=[pltpu.VMEM((d, tm), F32), pltpu.VMEM((te, tm), F32), pltpu.VMEM((te, tm), BF16)],
        compiler_params=_cparams("parallel", "arbitrary"),
        name="peer_main",
    )(xb, u, vt, th1, w1, s2, w2, x, g.reshape(1, d), b.reshape(1, d))


TOKEN_TILE = 1024
PROJ_COL_TILE = 512
OUT_K_TILE = 512
OUT_ROW_TILE = 512
PEER_TOKEN_TILE = 512
PEER_EXPERT_TILE = 512


def kernel(x_prompt, x_sample, state_gdn_conv, state_gdn_s, state_mlstm_c, state_mlstm_n, state_mlstm_m,
           gdn_w_in, gdn_conv_w, gdn_a_log, gdn_dt_bias, gdn_norm_w, gdn_w_out,
           ml_w_in, ml_gate_b, ml_norm_w, ml_w_out,
           ln_mix_g, ln_mix_b, ln_ffn_g, ln_ffn_b,
           peer_w_q, peer_keys, peer_u, peer_v):
    bp, tp, d = x_prompt.shape
    bs, ts, _ = x_sample.shape
    np_tok = bp * tp
    ns_tok = bs * ts
    x = jnp.concatenate([x_prompt.reshape(np_tok, d), x_sample.reshape(ns_tok, d)], axis=0)
    xb = x.astype(BF16)
    cp = min(GDN_CHUNK, tp)
    cs = min(GDN_CHUNK, ts)

    p_conv, p_s, p_c, p_n, p_m = [], [], [], [], []
    s_conv, s_s, s_c, s_n, s_m = [], [], [], [], []
    for layer in range(DEPTH):
        j = layer // 2
        if layer % 2 == 0:
            w_in = gdn_w_in[j]
            proj = matmul(xb, w_in, GDN_MAIN_DIM, tm=TOKEN_TILE, tn=PROJ_COL_TILE)
            gates, gates_t = gate_proj(xb, w_in[:, GDN_MAIN_DIM:], tm=TOKEN_TILE)
            shared = (gdn_conv_w[j], gdn_a_log[j], gdn_dt_bias[j], gdn_norm_w[j])
            zc = jnp.zeros((bp,) + state_gdn_conv.shape[2:], F32)
            zs = jnp.zeros((bp,) + state_gdn_s.shape[2:], F32)
            o_p, conv_p, st_p = gdn_group(proj, gates, gates_t, zc, zs, *shared, row0=0, nb=bp, t=tp, c=cp)
            o_s, conv_s, st_s = gdn_group(proj, gates, gates_t, state_gdn_conv[j], state_gdn_s[j], *shared,
                                          row0=np_tok, nb=bs, t=ts, c=cs)
            p_conv.append(conv_p), p_s.append(st_p), s_conv.append(conv_s), s_s.append(st_s)
            mix_in = jnp.concatenate([o_p, o_s], axis=0)
            w_out = gdn_w_out[j]
        else:
            w_in = ml_w_in[j]
            proj = matmul(xb, w_in, ML_MAIN_DIM, tm=TOKEN_TILE, tn=PROJ_COL_TILE)
            gates, gates_t = gate_proj(xb, w_in[:, ML_MAIN_DIM:], tm=TOKEN_TILE)
            shared = (ml_gate_b[j], ml_norm_w[j])
            zc = jnp.zeros((bp,) + state_mlstm_c.shape[2:], F32)
            zn = jnp.zeros((bp,) + state_mlstm_n.shape[2:], F32)
            zm = jnp.zeros((bp,) + state_mlstm_m.shape[2:], F32)
            o_p, c_p, n_p, m_p = mlstm_group(proj, gates, gates_t, zc, zn, zm, *shared, row0=0, nb=bp, t=tp, c=cp)
            o_s, c_s, n_s, m_s = mlstm_group(proj, gates, gates_t, state_mlstm_c[j], state_mlstm_n[j],
                                             state_mlstm_m[j], *shared, row0=np_tok, nb=bs, t=ts, c=cs)
            p_c.append(c_p), p_n.append(n_p), p_m.append(m_p), s_c.append(c_s), s_n.append(n_s), s_m.append(m_s)
            mix_in = jnp.concatenate([o_p, o_s], axis=0)
            w_out = ml_w_out[j]
        x, xb = out_proj_ln(mix_in, w_out, x, ln_mix_g[layer], ln_mix_b[layer], tm=OUT_ROW_TILE, tk=OUT_K_TILE)

        q = matmul(xb, peer_w_q[layer], peer_w_q.shape[2], tm=TOKEN_TILE, tn=PROJ_COL_TILE)
        tables = peer_route(q, peer_keys[layer], tm=PEER_TOKEN_TILE)
        x, xb = peer_main(xb, x, peer_u[layer].astype(BF16), peer_v[layer].astype(BF16).T, *tables,
                          ln_ffn_g[layer], ln_ffn_b[layer], tm=PEER_TOKEN_TILE, te=PEER_EXPERT_TILE)

    y_prompt = x[:np_tok].reshape(bp, tp, d)
    y_sample = x[np_tok:].reshape(bs, ts, d)
    return (y_prompt, y_sample,
            jnp.stack(p_conv), jnp.stack(p_s), jnp.stack(p_c), jnp.stack(p_n), jnp.stack(p_m),
            jnp.stack(s_conv), jnp.stack(s_s), jnp.stack(s_c), jnp.stack(s_n), jnp.stack(s_m))
```

```python
import functools

import jax
import jax.numpy as jnp
from jax import lax
from jax.experimental import pallas as pl
from jax.experimental.pallas import tpu as pltpu

F32 = jnp.float32
BF16 = jnp.bfloat16

D_MODEL = 2048
DEPTH = 4
GDN_K_HEADS = 16
GDN_V_HEADS = 32
GDN_DK = 128
GDN_DV = 128
GDN_QK_DIM = GDN_K_HEADS * GDN_DK
GDN_V_DIM = GDN_V_HEADS * GDN_DV
GDN_CONV_DIM = 2 * GDN_QK_DIM + GDN_V_DIM
GDN_MAIN_DIM = GDN_CONV_DIM + GDN_V_DIM
GDN_CONV_W = 4
GDN_CHUNK = 64
ML_HEADS = 8
ML_DQK = 128
ML_DV = 256
ML_QK_DIM = ML_HEADS * ML_DQK
ML_V_DIM = ML_HEADS * ML_DV
ML_MAIN_DIM = 2 * ML_QK_DIM + 2 * ML_V_DIM
ML_CHUNK = 64
PEER_HEADS = 8
PEER_N_KEYS = 128
PEER_HALF = 128
PEER_TOPK = 16
DEEPNORM_ALPHA = (2 * DEPTH) ** 0.25
LN_EPS = 1e-5
RMS_EPS = 1e-6
NEG_BIG = -1e30
NEG_INF = float("-inf")
POS_INF = float("inf")

VMEM_LIMIT_BYTES = 56 * 1024 * 1024
HIGHEST = lax.Precision.HIGHEST


def _cparams(*sem):
    return pltpu.CompilerParams(dimension_semantics=sem, vmem_limit_bytes=VMEM_LIMIT_BYTES)


def _dot(a, b):
    return jnp.dot(a.astype(BF16), b.astype(BF16), preferred_element_type=F32)


def _dot_nt(a, b):
    return lax.dot_general(a.astype(BF16), b.astype(BF16), (((1,), (1,)), ((), ())),
                           preferred_element_type=F32)


def _dot_tn(a, b):
    return lax.dot_general(a.astype(BF16), b.astype(BF16), (((0,), (0,)), ((), ())),
                           preferred_element_type=F32)


def _dot_f32(a, b):
    return jnp.dot(a, b, preferred_element_type=F32, precision=HIGHEST)


def _sigmoid(x):
    return 1.0 / (1.0 + jnp.exp(-x))


def _silu(x):
    return x * _sigmoid(x)


def _softplus(x):
    return jnp.maximum(x, 0.0) + jnp.log(1.0 + jnp.exp(-jnp.abs(x)))


def _log_sigmoid(x):
    return -_softplus(-x)


def _gelu_tanh(x):
    return 0.5 * x * (1.0 + jnp.tanh(0.7978845608028654 * (x + 0.044715 * (x * x * x))))


def _lower(c, strict):
    r = lax.broadcasted_iota(jnp.int32, (c, c), 0)
    k = lax.broadcasted_iota(jnp.int32, (c, c), 1)
    return (r > k) if strict else (r >= k)


def _upper(c):
    r = lax.broadcasted_iota(jnp.int32, (c, c), 0)
    k = lax.broadcasted_iota(jnp.int32, (c, c), 1)
    return r <= k


def _layer_norm_rows(v, g, b):
    mu = jnp.mean(v, axis=-1, keepdims=True)
    d = v - mu
    var = jnp.mean(d * d, axis=-1, keepdims=True)
    return d * lax.rsqrt(var + LN_EPS) * g + b


def _matmul_kernel(x_ref, w_ref, o_ref):
    o_ref[...] = _dot(x_ref[...], w_ref[...])


def matmul(x, w_stack, layer, n_cols, *, tm, tn):
    m, k = x.shape
    return pl.pallas_call(
        _matmul_kernel,
        grid=(m // tm, n_cols // tn),
        in_specs=[pl.BlockSpec((tm, k), lambda i, j: (i, 0)),
                  pl.BlockSpec((None, k, tn), lambda i, j: (layer, 0, j))],
        out_specs=pl.BlockSpec((tm, tn), lambda i, j: (i, j)),
        out_shape=jax.ShapeDtypeStruct((m, n_cols), F32),
        compiler_params=_cparams("parallel", "arbitrary"),
        name="proj_matmul",
    )(x, w_stack)


def _gate_proj_kernel(x_ref, w_ref, wt_ref, o_ref, ot_ref):
    x = x_ref[...]
    o_ref[...] = _dot(x, w_ref[...])
    ot_ref[...] = _dot_nt(wt_ref[...], x)


def gate_proj(x, w_gate, *, tm):
    m, k = x.shape
    n = w_gate.shape[1]
    return pl.pallas_call(
        _gate_proj_kernel,
        grid=(m // tm,),
        in_specs=[pl.BlockSpec((tm, k), lambda i: (i, 0)),
                  pl.BlockSpec((k, n), lambda i: (0, 0)),
                  pl.BlockSpec((n, k), lambda i: (0, 0))],
        out_specs=[pl.BlockSpec((tm, n), lambda i: (i, 0)),
                   pl.BlockSpec((n, tm), lambda i: (0, i))],
        out_shape=[jax.ShapeDtypeStruct((m, n), F32), jax.ShapeDtypeStruct((n, m), F32)],
        compiler_params=_cparams("parallel"),
        name="gate_proj",
    )(x, w_gate, w_gate.T)


def _out_proj_ln_kernel(hp_ref, hs_ref, w_ref, x_ref, g_ref, b_ref, o_ref, ob_ref, acc_ref, *, prompt_tiles):
    i = pl.program_id(0)
    kk = pl.program_id(1)

    @pl.when(kk == 0)
    def _():
        acc_ref[...] = jnp.zeros_like(acc_ref)

    @pl.when(i < prompt_tiles)
    def _():
        acc_ref[...] += _dot(hp_ref[...], w_ref[...])

    @pl.when(i >= prompt_tiles)
    def _():
        acc_ref[...] += _dot(hs_ref[...], w_ref[...])

    @pl.when(kk == pl.num_programs(1) - 1)
    def _():
        y = _layer_norm_rows(DEEPNORM_ALPHA * x_ref[...] + acc_ref[...], g_ref[...], b_ref[...])
        o_ref[...] = y
        ob_ref[...] = y.astype(BF16)


def out_proj_ln(h_prompt, h_sample, w_stack, layer, x, g, b, *, tm, tk):
    mp, k = h_prompt.shape
    ms = h_sample.shape[0]
    d = w_stack.shape[2]
    prompt_tiles = mp // tm
    last_k = k // tk - 1
    kernel = functools.partial(_out_proj_ln_kernel, prompt_tiles=prompt_tiles)
    return pl.pallas_call(
        kernel,
        grid=((mp + ms) // tm, k // tk),
        in_specs=[pl.BlockSpec((tm, tk), lambda i, j: (jnp.minimum(i, prompt_tiles - 1),
                                                        jnp.where(i < prompt_tiles, j, last_k))),
                  pl.BlockSpec((tm, tk), lambda i, j: (jnp.maximum(i - prompt_tiles, 0),
                                                        jnp.where(i < prompt_tiles, 0, j))),
                  pl.BlockSpec((None, tk, d), lambda i, j: (layer, j, 0)),
                  pl.BlockSpec((tm, d), lambda i, j: (i, 0)),
                  pl.BlockSpec((1, d), lambda i, j: (0, 0)),
                  pl.BlockSpec((1, d), lambda i, j: (0, 0))],
        out_specs=[pl.BlockSpec((tm, d), lambda i, j: (i, 0)),
                   pl.BlockSpec((tm, d), lambda i, j: (i, 0))],
        out_shape=[jax.ShapeDtypeStruct((mp + ms, d), F32), jax.ShapeDtypeStruct((mp + ms, d), BF16)],
        scratch_shapes=[pltpu.VMEM((tm, d), F32)],
        compiler_params=_cparams("parallel", "arbitrary"),
        name="out_proj_ln",
    )(h_prompt, h_sample, w_stack, x, g.reshape(1, d), b.reshape(1, d))


CONV_PAD = 8


def _dot_split(e, r):
    c = e.shape[0]
    e_hi = e.astype(BF16).astype(F32)
    r_hi = r.astype(BF16).astype(F32)
    stacked = _dot(jnp.concatenate([e_hi, e - e_hi], axis=0), r_hi)
    return stacked[0:c] + stacked[c:2 * c] + _dot(e_hi, r - r_hi)


def _gdn_kernel(qkv_ref, z_ref, bpre_ref, apre_ref, bpre_t_ref, apre_t_ref, conv0_ref, s0_ref,
                convw_ref, alog_ref, dtb_ref, alog_t_ref, dtb_t_ref, normw_ref, *rest, c):
    (o_ref, convo_ref, so_ref,
     xp_ref, s_ref, kq_ref, l_ref, a_ref, e_ref, p_ref, rhs_ref, lhs_ref, kd_ref, vn_ref) = rest[-14:]
    step = pl.program_id(1)
    hist = GDN_CONV_W - 1
    heads = range(GDN_V_HEADS)
    rep = GDN_V_HEADS // GDN_K_HEADS
    tok = slice(CONV_PAD, CONV_PAD + c)

    @pl.when(step == 0)
    def _():
        xp_ref[CONV_PAD - hist:CONV_PAD, :] = conv0_ref[0]
        s_ref[...] = s0_ref[0]

    x = qkv_ref[0]
    xp_ref[tok, :] = x
    y = xp_ref[CONV_PAD - hist:CONV_PAD - hist + c, :] * convw_ref[0:1, :]
    for i in range(1, GDN_CONV_W):
        y = y + xp_ref[CONV_PAD - hist + i:CONV_PAD - hist + i + c, :] * convw_ref[i:i + 1, :]
    tail = x[c - hist:c, :]
    xp_ref[CONV_PAD - hist:CONV_PAD, :] = tail
    convo_ref[0] = tail
    xp_ref[tok, :] = _silu(y)

    beta_c = _sigmoid(bpre_ref[0])
    g_c = -jnp.exp(alog_ref[...]) * _softplus(apre_ref[0] + dtb_ref[...])
    g_r = -jnp.exp(alog_t_ref[...]) * _softplus(apre_t_ref[0] + dtb_t_ref[...])
    rows = lax.broadcasted_iota(jnp.int32, (c, c), 0)
    cols = lax.broadcasted_iota(jnp.int32, (c, c), 1)
    causal = rows >= cols
    strict = rows > cols
    gc_c = _dot_f32(causal.astype(F32), g_c)
    gc_r = _dot_f32(g_r, (rows <= cols).astype(F32))

    for kh in range(GDN_K_HEADS):
        qs = slice(kh * GDN_DK, (kh + 1) * GDN_DK)
        ks = slice(GDN_QK_DIM + kh * GDN_DK, GDN_QK_DIM + (kh + 1) * GDN_DK)
        q = xp_ref[tok, qs]
        k = xp_ref[tok, ks]
        q = q * lax.rsqrt(jnp.sum(q * q, axis=-1, keepdims=True) + RMS_EPS) * (GDN_DK ** -0.5)
        k = k * lax.rsqrt(jnp.sum(k * k, axis=-1, keepdims=True) + RMS_EPS)
        xp_ref[tok, qs] = q
        xp_ref[tok, ks] = k
        kq_ref[kh] = _dot_nt(jnp.concatenate([k, q], axis=0), k)

    diag8 = strict & ((rows >> 3) == (cols >> 3))
    for h in heads:
        kh = h // rep
        q = xp_ref[tok, kh * GDN_DK:(kh + 1) * GDN_DK]
        k = xp_ref[tok, GDN_QK_DIM + kh * GDN_DK:GDN_QK_DIM + (kh + 1) * GDN_DK]
        v = xp_ref[tok, 2 * GDN_QK_DIM + h * GDN_DV:2 * GDN_QK_DIM + (h + 1) * GDN_DV]
        gcol = gc_c[:, h:h + 1]
        bcol = beta_c[:, h:h + 1]
        diff = gcol - gc_r[h:h + 1, :]
        egc = jnp.exp(gcol)
        l = bcol * kq_ref[kh, 0:c, :] * jnp.exp(jnp.where(strict, diff, NEG_BIG))
        l_ref[h] = l
        a_ref[h] = kq_ref[kh, c:2 * c, :] * jnp.exp(jnp.where(causal, diff, NEG_BIG))
        rhs_ref[h, :, 0:GDN_DV] = v * bcol
        rhs_ref[h, :, GDN_DV:GDN_DV + GDN_DK] = k * (bcol * egc)
        lhs_ref[h, c:2 * c, :] = q * egc
        kd_ref[h] = k * jnp.exp(gcol[c - 1:c, :] - gcol)
        l8 = jnp.where(diag8, l, 0.0)
        e_ref[h] = -l8
        p_ref[h] = _dot(l8, l8)
    for h in heads:
        e = e_ref[h]
        p = p_ref[h]
        e_ref[h] = e + p + _dot(e, p)
        p_ref[h] = _dot(p, p)
    for h in heads:
        e = e_ref[h]
        p = p_ref[h]
        e_ref[h] = e + p + _dot(e, p)
    shift = 3
    while (2 << shift) <= c:
        lower_left = (((rows >> (shift + 1)) == (cols >> (shift + 1)))
                      & (((rows >> shift) & 1) == 1) & (((cols >> shift) & 1) == 0))
        for h in heads:
            n = jnp.where(lower_left, l_ref[h], 0.0)
            p_ref[h] = n + _dot(e_ref[h], n)
        for h in heads:
            t = p_ref[h]
            e = e_ref[h]
            e_ref[h] = e - t - _dot(t, e)
        shift += 1

    for h in heads:
        r = rhs_ref[h]
        uw = r + _dot_split(e_ref[h], r)
        rhs_ref[h, :, 0:GDN_DV] = uw[:, 0:GDN_DV]
        lhs_ref[h, 0:c, :] = uw[:, GDN_DV:GDN_DV + GDN_DK]
    for h in heads:
        ws = _dot(lhs_ref[h], s_ref[h])
        vn_ref[h] = rhs_ref[h, :, 0:GDN_DV] - ws[0:c]
        lhs_ref[h, c:2 * c, :] = ws[c:2 * c]
    for h in heads:
        v_new = vn_ref[h]
        o = lhs_ref[h, c:2 * c, :] + _dot(a_ref[h], v_new)
        s_ref[h] = s_ref[h] * jnp.exp(gc_c[c - 1:c, h:h + 1]) + _dot_tn(kd_ref[h], v_new)
        o = o * lax.rsqrt(jnp.mean(o * o, axis=-1, keepdims=True) + RMS_EPS) * normw_ref[...]
        zh = z_ref[0, :, h * GDN_DV:(h + 1) * GDN_DV]
        o_ref[0, :, h * GDN_DV:(h + 1) * GDN_DV] = (o * _silu(zh)).astype(o_ref.dtype)

    @pl.when(step == pl.num_programs(1) - 1)
    def _():
        so_ref[0] = s_ref[...]


def _alias_args(prev, first_input, first_output):
    if prev is None:
        return [], [], {}
    specs = [pl.BlockSpec(memory_space=pl.ANY)] * len(prev)
    aliases = {first_input + i: first_output + i for i in range(len(prev))}
    return list(prev), specs, aliases


def gdn_group(proj, gates, gates_t, conv0, s0, in_slot, conv_w, a_log, dt_bias, norm_w, *, row0, nb, t, c,
              slot, n_slots, prev):
    tokens = proj.shape[0]
    nchunk = t // c
    blk0 = row0 // c
    hv = GDN_V_HEADS
    proj3 = proj.reshape(tokens // c, c, proj.shape[1])
    bpre = gates[:, :hv].reshape(tokens // c, c, hv)
    apre = gates[:, hv:].reshape(tokens // c, c, hv)
    bpre_t = gates_t[:hv].reshape(hv, tokens // c, c).transpose(1, 0, 2)
    apre_t = gates_t[hv:].reshape(hv, tokens // c, c).transpose(1, 0, 2)
    z_blk = GDN_CONV_DIM // GDN_V_DIM

    def rows(b, s):
        return blk0 + b * nchunk + s

    kernel = functools.partial(_gdn_kernel, c=c)
    prev_args, prev_specs, aliases = _alias_args(prev, first_input=14, first_output=1)
    o, conv, s = pl.pallas_call(
        kernel,
        grid=(nb, nchunk),
        input_output_aliases=aliases,
        in_specs=[
            pl.BlockSpec((1, c, GDN_CONV_DIM), lambda b, s: (rows(b, s), 0, 0)),
            pl.BlockSpec((1, c, GDN_V_DIM), lambda b, s: (rows(b, s), 0, z_blk)),
            pl.BlockSpec((1, c, hv), lambda b, s: (rows(b, s), 0, 0)),
            pl.BlockSpec((1, c, hv), lambda b, s: (rows(b, s), 0, 0)),
            pl.BlockSpec((1, hv, c), lambda b, s: (rows(b, s), 0, 0)),
            pl.BlockSpec((1, hv, c), lambda b, s: (rows(b, s), 0, 0)),
            pl.BlockSpec((None, 1, GDN_CONV_W - 1, GDN_CONV_DIM), lambda b, s: (in_slot, b, 0, 0)),
            pl.BlockSpec((None, 1, hv, GDN_DK, GDN_DV), lambda b, s: (in_slot, b, 0, 0, 0)),
            pl.BlockSpec((GDN_CONV_W, GDN_CONV_DIM), lambda b, s: (0, 0)),
            pl.BlockSpec((1, hv), lambda b, s: (0, 0)),
            pl.BlockSpec((1, hv), lambda b, s: (0, 0)),
            pl.BlockSpec((hv, 1), lambda b, s: (0, 0)),
            pl.BlockSpec((hv, 1), lambda b, s: (0, 0)),
            pl.BlockSpec((1, GDN_DV), lambda b, s: (0, 0)),
        ] + prev_specs,
        out_specs=[
            pl.BlockSpec((1, c, GDN_V_DIM), lambda b, s: (b * nchunk + s, 0, 0)),
            pl.BlockSpec((None, 1, GDN_CONV_W - 1, GDN_CONV_DIM), lambda b, s: (slot, b, 0, 0)),
            pl.BlockSpec((None, 1, hv, GDN_DK, GDN_DV), lambda b, s: (slot, b, 0, 0, 0)),
        ],
        out_shape=[
            jax.ShapeDtypeStruct((nb * nchunk, c, GDN_V_DIM), BF16),
            jax.ShapeDtypeStruct((n_slots, nb, GDN_CONV_W - 1, GDN_CONV_DIM), F32),
            jax.ShapeDtypeStruct((n_slots, nb, hv, GDN_DK, GDN_DV), F32),
        ],
        scratch_shapes=[pltpu.VMEM((CONV_PAD + c, GDN_CONV_DIM), F32),
                        pltpu.VMEM((hv, GDN_DK, GDN_DV), F32),
                        pltpu.VMEM((GDN_K_HEADS, 2 * c, c), F32),
                        pltpu.VMEM((hv, c, c), F32),
                        pltpu.VMEM((hv, c, c), F32),
                        pltpu.VMEM((hv, c, c), F32),
                        pltpu.VMEM((hv, c, c), F32),
                        pltpu.VMEM((hv, c, GDN_DV + GDN_DK), F32),
                        pltpu.VMEM((hv, 2 * c, GDN_DK), F32),
                        pltpu.VMEM((hv, c, GDN_DK), F32),
                        pltpu.VMEM((hv, c, GDN_DV), F32)],
        compiler_params=_cparams("parallel", "arbitrary"),
        name=f"gdn_chunk{c}",
    )(proj3, proj3, bpre, apre, bpre_t, apre_t, conv0, s0, conv_w,
      a_log.reshape(1, hv), dt_bias.reshape(1, hv), a_log.reshape(hv, 1), dt_bias.reshape(hv, 1),
      norm_w.reshape(1, GDN_DV), *prev_args)
    return o.reshape(nb * t, GDN_V_DIM), (conv, s)


def _mlstm_kernel(main_ref, ig_ref, fg_ref, ig_t_ref, fg_t_ref, c0_ref, n0_ref, m0_ref,
                  bi_ref, bf_ref, bi_t_ref, bf_t_ref, normw_ref, *rest, c):
    o_ref, co_ref, no_ref, mo_ref, c_ref, n_ref, m_ref = rest[-7:]
    step = pl.program_id(1)

    @pl.when(step == 0)
    def _():
        c_ref[...] = c0_ref[0]
        n_ref[...] = n0_ref[0]
        m_ref[...] = jnp.broadcast_to(m0_ref[0], m_ref.shape)

    ig_c = ig_ref[0] + bi_ref[...]
    lf_c = _log_sigmoid(fg_ref[0] + bf_ref[...])
    ig_r = ig_t_ref[0] + bi_t_ref[...]
    lf_r = _log_sigmoid(fg_t_ref[0] + bf_t_ref[...])
    causal = _lower(c, strict=False)
    bc_c = _dot_f32(causal.astype(F32), lf_c)
    bc_r = _dot_f32(lf_r, _upper(c).astype(F32))

    for h in range(ML_HEADS):
        q = main_ref[0, :, h * ML_DQK:(h + 1) * ML_DQK]
        k = main_ref[0, :, ML_QK_DIM + h * ML_DQK:ML_QK_DIM + (h + 1) * ML_DQK] * (ML_DQK ** -0.5)
        v = main_ref[0, :, 2 * ML_QK_DIM + h * ML_DV:2 * ML_QK_DIM + (h + 1) * ML_DV]
        o_pre = main_ref[0, :, 2 * ML_QK_DIM + ML_V_DIM + h * ML_DV:2 * ML_QK_DIM + ML_V_DIM + (h + 1) * ML_DV]
        bcol = bc_c[:, h:h + 1]
        brow = bc_r[h:h + 1, :]
        icol = ig_c[:, h:h + 1]
        irow = ig_r[h:h + 1, :]
        m_prev = m_ref[h:h + 1, 0:1]
        cm = c_ref[h]
        nv = n_ref[h:h + 1, :]

        d = jnp.where(causal, bcol - brow + irow, NEG_BIG)
        inter = bcol + m_prev
        m_t = jnp.maximum(inter, jnp.max(d, axis=-1, keepdims=True))
        w_intra = jnp.exp(d - m_t)
        w_inter = jnp.exp(inter - m_t)
        sqk = _dot_nt(q, k) * w_intra
        num = w_inter * _dot(q, cm) + _dot(sqk, v)
        den = w_inter * jnp.sum(q * nv, axis=-1, keepdims=True) + jnp.sum(sqk, axis=-1, keepdims=True)
        hid = num / jnp.maximum(jnp.abs(den), jnp.exp(-m_t))

        b_last = bcol[c - 1:c, :]
        d_end = b_last - bcol + icol
        m_new = jnp.maximum(b_last + m_prev, jnp.max(d_end, axis=0, keepdims=True))
        wk = jnp.exp(d_end - m_new) * k
        scale = jnp.exp(b_last + m_prev - m_new)
        c_ref[h] = scale * cm + _dot_tn(wk, v)
        n_ref[h:h + 1, :] = scale * nv + jnp.sum(wk, axis=0, keepdims=True)
        m_ref[h:h + 1, :] = jnp.broadcast_to(m_new, (1, m_ref.shape[1]))

        hid = hid * lax.rsqrt(jnp.mean(hid * hid, axis=-1, keepdims=True) + RMS_EPS) * normw_ref[...]
        o_ref[0, :, h * ML_DV:(h + 1) * ML_DV] = (hid * _sigmoid(o_pre)).astype(o_ref.dtype)

    @pl.when(step == pl.num_programs(1) - 1)
    def _():
        co_ref[0] = c_ref[...]
        no_ref[0] = n_ref[...]
        mo_ref[0] = m_ref[:, 0:1]


def mlstm_group(proj, gates, gates_t, c0, n0, m0, in_slot, gate_b, norm_w, *, row0, nb, t, c,
                slot, n_slots, prev):
    tokens = proj.shape[0]
    nchunk = t // c
    blk0 = row0 // c
    nh = ML_HEADS
    proj3 = proj.reshape(tokens // c, c, proj.shape[1])
    ig = gates[:, :nh].reshape(tokens // c, c, nh)
    fg = gates[:, nh:].reshape(tokens // c, c, nh)
    ig_t = gates_t[:nh].reshape(nh, tokens // c, c).transpose(1, 0, 2)
    fg_t = gates_t[nh:].reshape(nh, tokens // c, c).transpose(1, 0, 2)

    def rows(b, s):
        return blk0 + b * nchunk + s

    kernel = functools.partial(_mlstm_kernel, c=c)
    prev_args, prev_specs, aliases = _alias_args(prev, first_input=13, first_output=1)
    o, cm, nv, m = pl.pallas_call(
        kernel,
        grid=(nb, nchunk),
        input_output_aliases=aliases,
        in_specs=[
            pl.BlockSpec((1, c, ML_MAIN_DIM), lambda b, s: (rows(b, s), 0, 0)),
            pl.BlockSpec((1, c, nh), lambda b, s: (rows(b, s), 0, 0)),
            pl.BlockSpec((1, c, nh), lambda b, s: (rows(b, s), 0, 0)),
            pl.BlockSpec((1, nh, c), lambda b, s: (rows(b, s), 0, 0)),
            pl.BlockSpec((1, nh, c), lambda b, s: (rows(b, s), 0, 0)),
            pl.BlockSpec((None, 1, nh, ML_DQK, ML_DV), lambda b, s: (in_slot, b, 0, 0, 0)),
            pl.BlockSpec((None, 1, nh, ML_DQK), lambda b, s: (in_slot, b, 0, 0)),
            pl.BlockSpec((None, 1, nh, 1), lambda b, s: (in_slot, b, 0, 0)),
            pl.BlockSpec((1, nh), lambda b, s: (0, 0)),
            pl.BlockSpec((1, nh), lambda b, s: (0, 0)),
            pl.BlockSpec((nh, 1), lambda b, s: (0, 0)),
            pl.BlockSpec((nh, 1), lambda b, s: (0, 0)),
            pl.BlockSpec((1, ML_DV), lambda b, s: (0, 0)),
        ] + prev_specs,
        out_specs=[
            pl.BlockSpec((1, c, ML_V_DIM), lambda b, s: (b * nchunk + s, 0, 0)),
            pl.BlockSpec((None, 1, nh, ML_DQK, ML_DV), lambda b, s: (slot, b, 0, 0, 0)),
            pl.BlockSpec((None, 1, nh, ML_DQK), lambda b, s: (slot, b, 0, 0)),
            pl.BlockSpec((None, 1, nh, 1), lambda b, s: (slot, b, 0, 0)),
        ],
        out_shape=[
            jax.ShapeDtypeStruct((nb * nchunk, c, ML_V_DIM), BF16),
            jax.ShapeDtypeStruct((n_slots, nb, nh, ML_DQK, ML_DV), F32),
            jax.ShapeDtypeStruct((n_slots, nb, nh, ML_DQK), F32),
            jax.ShapeDtypeStruct((n_slots, nb, nh, 1), F32),
        ],
        scratch_shapes=[pltpu.VMEM((nh, ML_DQK, ML_DV), F32),
                        pltpu.VMEM((nh, ML_DQK), F32),
                        pltpu.VMEM((nh, 128), F32)],
        compiler_params=_cparams("parallel", "arbitrary"),
        name=f"mlstm_chunk{c}",
    )(proj3, ig, fg, ig_t, fg_t, c0, n0, m0.reshape(m0.shape + (1,)),
      gate_b[:nh].reshape(1, nh), gate_b[nh:].reshape(1, nh),
      gate_b[:nh].reshape(nh, 1), gate_b[nh:].reshape(nh, 1), norm_w.reshape(1, ML_DV), *prev_args)
    return o.reshape(nb * t, ML_V_DIM), (cm, nv, m)


_CAND_PAIRS = [(a, b) for a in range(PEER_TOPK) for b in range(PEER_TOPK) if (a + 1) * (b + 1) <= PEER_TOPK]
_CAND_ROWS = -(-len(_CAND_PAIRS) // 8) * 8


def _top_values(work, count):
    rows = lax.broadcasted_iota(jnp.int32, work.shape, 0)
    out = []
    for r in range(count):
        m = jnp.max(work, axis=0, keepdims=True)
        out.append(m)
        if r + 1 < count:
            first = jnp.min(jnp.where(work == m, rows, work.shape[0]), axis=0, keepdims=True)
            work = jnp.where(rows == first, NEG_INF, work)
    return out


def _peer_route_kernel(q_ref, keys_ref, th1_ref, w1_ref, s2_ref, w2_ref, cand_ref):
    cand_ref[...] = jnp.full(cand_ref.shape, NEG_INF, F32)
    for h in range(PEER_HEADS):
        scores, tops = [], []
        for p in range(2):
            col = (2 * h + p) * PEER_HALF
            s = _dot_nt(keys_ref[h, p], q_ref[:, col:col + PEER_HALF])
            scores.append(s)
            tops.append(_top_values(s, PEER_TOPK))
        for i, (a, b) in enumerate(_CAND_PAIRS):
            cand_ref[i:i + 1, :] = tops[0][a] + tops[1][b]
        cand = cand_ref[...]
        tau = _top_values(cand, PEER_TOPK)[-1]
        max1, max2 = tops[0][0], tops[1][0]
        z = jnp.sum(jnp.where(cand >= tau, jnp.exp(cand - (max1 + max2)), 0.0), axis=0, keepdims=True)
        cut = jnp.full(scores[0].shape, POS_INF, F32)
        for a in range(PEER_TOPK):
            cut_a = jnp.full(tau.shape, POS_INF, F32)
            for b in range(PEER_TOPK // (a + 1)):
                cut_a = jnp.where(tops[0][a] + tops[1][b] >= tau, tops[1][b], cut_a)
            cut = jnp.where(scores[0] == tops[0][a], cut_a, cut)
        th1_ref[h] = cut
        w1_ref[h] = jnp.exp(scores[0] - max1)
        s2_ref[h] = scores[1]
        w2_ref[h] = jnp.exp(scores[1] - max2) / z


def peer_route(q, keys, *, tm):
    m = q.shape[0]
    nh = PEER_HEADS
    table = jax.ShapeDtypeStruct((nh, PEER_N_KEYS, m), F32)
    table_spec = pl.BlockSpec((nh, PEER_N_KEYS, tm), lambda i: (0, 0, i))
    return pl.pallas_call(
        _peer_route_kernel,
        grid=(m // tm,),
        in_specs=[pl.BlockSpec((tm, q.shape[1]), lambda i: (i, 0)),
                  pl.BlockSpec(keys.shape, lambda i: (0, 0, 0, 0))],
        out_specs=[table_spec] * 4,
        out_shape=[table] * 4,
        scratch_shapes=[pltpu.VMEM((_CAND_ROWS, tm), F32)],
        compiler_params=_cparams("parallel"),
        name="peer_route",
    )(q, keys)


def _table_cast_kernel(u_ref, v_ref, ub_ref, vt_ref):
    ub_ref[...] = u_ref[...].astype(BF16)
    vt_ref[...] = v_ref[...].T.astype(BF16)


def cast_tables(u_stack, v_stack, layer, *, rows):
    ne, d = u_stack.shape[1:]
    return pl.pallas_call(
        _table_cast_kernel,
        grid=(ne // rows,),
        in_specs=[pl.BlockSpec((None, rows, d), lambda i: (layer, i, 0)),
                  pl.BlockSpec((None, rows, d), lambda i: (layer, i, 0))],
        out_specs=[pl.BlockSpec((rows, d), lambda i: (i, 0)),
                   pl.BlockSpec((d, rows), lambda i: (0, i))],
        out_shape=[jax.ShapeDtypeStruct((ne, d), BF16), jax.ShapeDtypeStruct((d, ne), BF16)],
        compiler_params=_cparams("parallel"),
        name="peer_table_cast",
    )(u_stack, v_stack)


LANES = 128


def _peer_main_kernel(xb_ref, u_ref, vt_ref, th1_ref, w1_ref, s2_ref, w2_ref, x_ref, g_ref, b_ref,
                      o_ref, ob_ref, acc_ref, ht_ref, a_ref, *, te):
    e = pl.program_id(1)
    groups = te // PEER_N_KEYS
    tm = ht_ref.shape[1]

    @pl.when(e == 0)
    def _():
        acc_ref[...] = jnp.zeros_like(acc_ref)

    ht_ref[...] = _dot_nt(u_ref[...], xb_ref[...])
    for cc in range(groups):
        key1 = e * groups + cc
        rows = slice(cc * PEER_N_KEYS, (cc + 1) * PEER_N_KEYS)
        for t0 in range(0, tm, 2 * LANES):
            pair = slice(t0, t0 + 2 * LANES)
            th_rows = [th1_ref[h, pl.ds(key1, 1), pair] for h in range(PEER_HEADS)]
            w1_rows = [w1_ref[h, pl.ds(key1, 1), pair] for h in range(PEER_HEADS)]
            for half in range(2):
                lanes = slice(t0 + half * LANES, t0 + (half + 1) * LANES)
                sub = slice(half * LANES, (half + 1) * LANES)
                gate = jnp.zeros((PEER_N_KEYS, LANES), F32)
                for h in range(PEER_HEADS):
                    hit = s2_ref[h, :, lanes] >= th_rows[h][:, sub]
                    gate = gate + jnp.where(hit, w2_ref[h, :, lanes], 0.0) * w1_rows[h][:, sub]
                a_ref[rows, lanes] = (_gelu_tanh(ht_ref[rows, lanes]) * gate).astype(BF16)
    acc_ref[...] += _dot(vt_ref[...], a_ref[...])

    @pl.when(e == pl.num_programs(1) - 1)
    def _():
        for t0 in range(0, tm, LANES):
            tok = slice(t0, t0 + LANES)
            y = _layer_norm_rows(DEEPNORM_ALPHA * x_ref[tok, :] + acc_ref[:, tok].T, g_ref[...], b_ref[...])
            o_ref[tok, :] = y
            ob_ref[tok, :] = y.astype(BF16)


def peer_main(xb, x, u, vt, th1, w1, s2, w2, g, b, *, tm, te):
    m, d = x.shape
    ne = u.shape[0]
    nh = PEER_HEADS
    kernel = functools.partial(_peer_main_kernel, te=te)
    once = pl.Buffered(1)
    table_spec = pl.BlockSpec((nh, PEER_N_KEYS, tm), lambda i, e: (0, 0, i), pipeline_mode=once)
    return pl.pallas_call(
        kernel,
        grid=(m // tm, ne // te),
        in_specs=[pl.BlockSpec((tm, d), lambda i, e: (i, 0), pipeline_mode=once),
                  pl.BlockSpec((te, d), lambda i, e: (e, 0)),
                  pl.BlockSpec((d, te), lambda i, e: (0, e)),
                  table_spec, table_spec, table_spec, table_spec,
                  pl.BlockSpec((tm, d), lambda i, e: (i, 0), pipeline_mode=once),
                  pl.BlockSpec((1, d), lambda i, e: (0, 0)),
                  pl.BlockSpec((1, d), lambda i, e: (0, 0))],
        out_specs=[pl.BlockSpec((tm, d), lambda i, e: (i, 0), pipeline_mode=once),
                   pl.BlockSpec((tm, d), lambda i, e: (i, 0), pipeline_mode=once)],
        out_shape=[jax.ShapeDtypeStruct((m, d), F32), jax.ShapeDtypeStruct((m, d), BF16)],
        scratch_shapes=[pltpu.VMEM((d, tm), F32), pltpu.VMEM((te, tm), F32), pltpu.VMEM((te, tm), BF16)],
        compiler_params=_cparams("parallel", "arbitrary"),
        name="peer_main",
    )(xb, u, vt, th1, w1, s2, w2, x, g.reshape(1, d), b.reshape(1, d))


TOKEN_TILE = 1024
PROJ_COL_TILE = 512
OUT_K_TILE = 512
OUT_ROW_TILE = 512
PEER_TOKEN_TILE = 512
PEER_EXPERT_TILE = 1024
TABLE_CAST_ROWS = 512


def kernel(x_prompt, x_sample, state_gdn_conv, state_gdn_s, state_mlstm_c, state_mlstm_n, state_mlstm_m,
           gdn_w_in, gdn_conv_w, gdn_a_log, gdn_dt_bias, gdn_norm_w, gdn_w_out,
           ml_w_in, ml_gate_b, ml_norm_w, ml_w_out,
           ln_mix_g, ln_mix_b, ln_ffn_g, ln_ffn_b,
           peer_w_q, peer_keys, peer_u, peer_v):
    bp, tp, d = x_prompt.shape
    bs, ts, _ = x_sample.shape
    np_tok = bp * tp
    ns_tok = bs * ts
    x = jnp.concatenate([x_prompt.reshape(np_tok, d), x_sample.reshape(ns_tok, d)], axis=0)
    xb = x.astype(BF16)
    cp = min(GDN_CHUNK, tp)
    cs = min(GDN_CHUNK, ts)

    n_gdn = state_gdn_s.shape[0]
    n_ml = state_mlstm_c.shape[0]
    zero_gdn = (jnp.zeros((1, bp) + state_gdn_conv.shape[2:], F32), jnp.zeros((1, bp) + state_gdn_s.shape[2:], F32))
    zero_ml = (jnp.zeros((1, bp) + state_mlstm_c.shape[2:], F32), jnp.zeros((1, bp) + state_mlstm_n.shape[2:], F32),
               jnp.zeros((1, bp) + state_mlstm_m.shape[2:], F32))
    p_gdn = s_gdn = p_ml = s_ml = None
    for layer in range(DEPTH):
        j = layer // 2
        prompt = dict(row0=0, nb=bp, t=tp, c=cp, slot=j)
        sample = dict(row0=np_tok, nb=bs, t=ts, c=cs, slot=j)
        if layer % 2 == 0:
            proj = matmul(xb, gdn_w_in, j, GDN_MAIN_DIM, tm=TOKEN_TILE, tn=PROJ_COL_TILE)
            gates, gates_t = gate_proj(xb, gdn_w_in[j, :, GDN_MAIN_DIM:], tm=TOKEN_TILE)
            shared = (gdn_conv_w[j], gdn_a_log[j], gdn_dt_bias[j], gdn_norm_w[j])
            o_p, p_gdn = gdn_group(proj, gates, gates_t, *zero_gdn, 0, *shared, n_slots=n_gdn, prev=p_gdn, **prompt)
            o_s, s_gdn = gdn_group(proj, gates, gates_t, state_gdn_conv, state_gdn_s, j, *shared,
                                   n_slots=n_gdn, prev=s_gdn, **sample)
            w_out = gdn_w_out
        else:
            proj = matmul(xb, ml_w_in, j, ML_MAIN_DIM, tm=TOKEN_TILE, tn=PROJ_COL_TILE)
            gates, gates_t = gate_proj(xb, ml_w_in[j, :, ML_MAIN_DIM:], tm=TOKEN_TILE)
            shared = (ml_gate_b[j], ml_norm_w[j])
            o_p, p_ml = mlstm_group(proj, gates, gates_t, *zero_ml, 0, *shared, n_slots=n_ml, prev=p_ml, **prompt)
            o_s, s_ml = mlstm_group(proj, gates, gates_t, state_mlstm_c, state_mlstm_n, state_mlstm_m, j, *shared,
                                    n_slots=n_ml, prev=s_ml, **sample)
            w_out = ml_w_out
        x, xb = out_proj_ln(o_p, o_s, w_out, j, x, ln_mix_g[layer], ln_mix_b[layer],
                            tm=OUT_ROW_TILE, tk=OUT_K_TILE)

        q = matmul(xb, peer_w_q, layer, peer_w_q.shape[2], tm=TOKEN_TILE, tn=PROJ_COL_TILE)
        tables = peer_route(q, peer_keys[layer], tm=PEER_TOKEN_TILE)
        u_bf16, vt_bf16 = cast_tables(peer_u, peer_v, layer, rows=TABLE_CAST_ROWS)
        x, xb = peer_main(xb, x, u_bf16, vt_bf16, *tables,
                          ln_ffn_g[layer], ln_ffn_b[layer], tm=PEER_TOKEN_TILE, te=PEER_EXPERT_TILE)

    y_prompt = x[:np_tok].reshape(bp, tp, d)
    y_sample = x[np_tok:].reshape(bs, ts, d)
    return (y_prompt, y_sample,
            p_gdn[0], p_gdn[1], p_ml[0], p_ml[1], p_ml[2].reshape(n_ml, bp, ML_HEADS),
            s_gdn[0], s_gdn[1], s_ml[0], s_ml[1], s_ml[2].reshape(n_ml, bs, ML_HEADS))
```

```python
import functools

import jax
import jax.numpy as jnp
from jax import lax
from jax.experimental import pallas as pl
from jax.experimental.pallas import tpu as pltpu

F32 = jnp.float32
BF16 = jnp.bfloat16

D_MODEL = 2048
DEPTH = 4
GDN_K_HEADS = 16
GDN_V_HEADS = 32
GDN_DK = 128
GDN_DV = 128
GDN_QK_DIM = GDN_K_HEADS * GDN_DK
GDN_V_DIM = GDN_V_HEADS * GDN_DV
GDN_CONV_DIM = 2 * GDN_QK_DIM + GDN_V_DIM
GDN_MAIN_DIM = GDN_CONV_DIM + GDN_V_DIM
GDN_CONV_W = 4
GDN_CHUNK = 64
ML_HEADS = 8
ML_DQK = 128
ML_DV = 256
ML_QK_DIM = ML_HEADS * ML_DQK
ML_V_DIM = ML_HEADS * ML_DV
ML_MAIN_DIM = 2 * ML_QK_DIM + 2 * ML_V_DIM
ML_CHUNK = 64
PEER_HEADS = 8
PEER_N_KEYS = 128
PEER_HALF = 128
PEER_TOPK = 16
DEEPNORM_ALPHA = (2 * DEPTH) ** 0.25
LN_EPS = 1e-5
RMS_EPS = 1e-6
NEG_BIG = -1e30
NEG_INF = float("-inf")
POS_INF = float("inf")

VMEM_LIMIT_BYTES = 56 * 1024 * 1024
HIGHEST = lax.Precision.HIGHEST


def _cparams(*sem):
    return pltpu.CompilerParams(dimension_semantics=sem, vmem_limit_bytes=VMEM_LIMIT_BYTES)


def _dot(a, b):
    return jnp.dot(a.astype(BF16), b.astype(BF16), preferred_element_type=F32)


def _dot_nt(a, b):
    return lax.dot_general(a.astype(BF16), b.astype(BF16), (((1,), (1,)), ((), ())),
                           preferred_element_type=F32)


def _dot_tn(a, b):
    return lax.dot_general(a.astype(BF16), b.astype(BF16), (((0,), (0,)), ((), ())),
                           preferred_element_type=F32)


def _dot_f32(a, b):
    return jnp.dot(a, b, preferred_element_type=F32, precision=HIGHEST)


def _sigmoid(x):
    return 1.0 / (1.0 + jnp.exp(-x))


def _silu(x):
    return x * _sigmoid(x)


def _softplus(x):
    return jnp.maximum(x, 0.0) + jnp.log(1.0 + jnp.exp(-jnp.abs(x)))


def _log_sigmoid(x):
    return -_softplus(-x)


def _gelu_tanh(x):
    return 0.5 * x * (1.0 + jnp.tanh(0.7978845608028654 * (x + 0.044715 * (x * x * x))))


def _lower(c, strict):
    r = lax.broadcasted_iota(jnp.int32, (c, c), 0)
    k = lax.broadcasted_iota(jnp.int32, (c, c), 1)
    return (r > k) if strict else (r >= k)


def _upper(c):
    r = lax.broadcasted_iota(jnp.int32, (c, c), 0)
    k = lax.broadcasted_iota(jnp.int32, (c, c), 1)
    return r <= k


def _layer_norm_rows(v, g, b):
    mu = jnp.mean(v, axis=-1, keepdims=True)
    d = v - mu
    var = jnp.mean(d * d, axis=-1, keepdims=True)
    return d * lax.rsqrt(var + LN_EPS) * g + b


def _matmul_kernel(x_ref, w_ref, o_ref):
    o_ref[...] = _dot(x_ref[...], w_ref[...])


def matmul(x, w_stack, layer, n_cols, *, tm, tn):
    m, k = x.shape
    return pl.pallas_call(
        _matmul_kernel,
        grid=(m // tm, n_cols // tn),
        in_specs=[pl.BlockSpec((tm, k), lambda i, j: (i, 0), pipeline_mode=pl.Buffered(1)),
                  pl.BlockSpec((None, k, tn), lambda i, j: (layer, 0, j))],
        out_specs=pl.BlockSpec((tm, tn), lambda i, j: (i, j)),
        out_shape=jax.ShapeDtypeStruct((m, n_cols), F32),
        compiler_params=_cparams("parallel", "arbitrary"),
        name="proj_matmul",
    )(x, w_stack)


def _gate_proj_kernel(x_ref, w_ref, wt_ref, o_ref, ot_ref):
    x = x_ref[...]
    o_ref[...] = _dot(x, w_ref[...])
    ot_ref[...] = _dot_nt(wt_ref[...], x)


def gate_proj(x, w_gate, *, tm):
    m, k = x.shape
    n = w_gate.shape[1]
    return pl.pallas_call(
        _gate_proj_kernel,
        grid=(m // tm,),
        in_specs=[pl.BlockSpec((tm, k), lambda i: (i, 0)),
                  pl.BlockSpec((k, n), lambda i: (0, 0)),
                  pl.BlockSpec((n, k), lambda i: (0, 0))],
        out_specs=[pl.BlockSpec((tm, n), lambda i: (i, 0)),
                   pl.BlockSpec((n, tm), lambda i: (0, i))],
        out_shape=[jax.ShapeDtypeStruct((m, n), F32), jax.ShapeDtypeStruct((n, m), F32)],
        compiler_params=_cparams("parallel"),
        name="gate_proj",
    )(x, w_gate, w_gate.T)


def _out_proj_ln_kernel(hp_ref, hs_ref, w_ref, x_ref, g_ref, b_ref, o_ref, ob_ref, acc_ref, *, prompt_tiles):
    i = pl.program_id(0)
    kk = pl.program_id(1)

    @pl.when(kk == 0)
    def _():
        acc_ref[...] = jnp.zeros_like(acc_ref)

    @pl.when(i < prompt_tiles)
    def _():
        acc_ref[...] += _dot(hp_ref[...], w_ref[...])

    @pl.when(i >= prompt_tiles)
    def _():
        acc_ref[...] += _dot(hs_ref[...], w_ref[...])

    @pl.when(kk == pl.num_programs(1) - 1)
    def _():
        y = _layer_norm_rows(DEEPNORM_ALPHA * x_ref[...] + acc_ref[...], g_ref[...], b_ref[...])
        o_ref[...] = y
        ob_ref[...] = y.astype(BF16)


def out_proj_ln(h_prompt, h_sample, w_stack, layer, x, g, b, *, tm, tk):
    mp, k = h_prompt.shape
    ms = h_sample.shape[0]
    d = w_stack.shape[2]
    prompt_tiles = mp // tm
    last_k = k // tk - 1
    kernel = functools.partial(_out_proj_ln_kernel, prompt_tiles=prompt_tiles)
    return pl.pallas_call(
        kernel,
        grid=((mp + ms) // tm, k // tk),
        in_specs=[pl.BlockSpec((tm, tk), lambda i, j: (jnp.minimum(i, prompt_tiles - 1),
                                                        jnp.where(i < prompt_tiles, j, last_k))),
                  pl.BlockSpec((tm, tk), lambda i, j: (jnp.maximum(i - prompt_tiles, 0),
                                                        jnp.where(i < prompt_tiles, 0, j))),
                  pl.BlockSpec((None, tk, d), lambda i, j: (layer, j, 0)),
                  pl.BlockSpec((tm, d), lambda i, j: (i, 0), pipeline_mode=pl.Buffered(1)),
                  pl.BlockSpec((1, d), lambda i, j: (0, 0)),
                  pl.BlockSpec((1, d), lambda i, j: (0, 0))],
        out_specs=[pl.BlockSpec((tm, d), lambda i, j: (i, 0), pipeline_mode=pl.Buffered(1)),
                   pl.BlockSpec((tm, d), lambda i, j: (i, 0), pipeline_mode=pl.Buffered(1))],
        out_shape=[jax.ShapeDtypeStruct((mp + ms, d), F32), jax.ShapeDtypeStruct((mp + ms, d), BF16)],
        scratch_shapes=[pltpu.VMEM((tm, d), F32)],
        compiler_params=_cparams("parallel", "arbitrary"),
        name="out_proj_ln",
    )(h_prompt, h_sample, w_stack, x, g.reshape(1, d), b.reshape(1, d))


CONV_PAD = 8


def _dot_split(e, r):
    c = e.shape[0]
    e_hi = e.astype(BF16).astype(F32)
    r_hi = r.astype(BF16).astype(F32)
    stacked = _dot(jnp.concatenate([e_hi, e - e_hi], axis=0), r_hi)
    return stacked[0:c] + stacked[c:2 * c] + _dot(e_hi, r - r_hi)


def _gdn_kernel(qkv_ref, z_ref, bpre_ref, apre_ref, bpre_t_ref, apre_t_ref, conv0_ref, s0_ref,
                convw_ref, alog_ref, dtb_ref, alog_t_ref, dtb_t_ref, normw_ref, *rest, c):
    (o_ref, convo_ref, so_ref,
     xp_ref, s_ref, kq_ref, l_ref, a_ref, e_ref, p_ref, rhs_ref, lhs_ref, kd_ref, vn_ref) = rest[-14:]
    step = pl.program_id(1)
    hist = GDN_CONV_W - 1
    heads = range(GDN_V_HEADS)
    rep = GDN_V_HEADS // GDN_K_HEADS
    tok = slice(CONV_PAD, CONV_PAD + c)

    @pl.when(step == 0)
    def _():
        xp_ref[CONV_PAD - hist:CONV_PAD, :] = conv0_ref[0]
        s_ref[...] = s0_ref[0]

    x = qkv_ref[0]
    xp_ref[tok, :] = x
    y = xp_ref[CONV_PAD - hist:CONV_PAD - hist + c, :] * convw_ref[0:1, :]
    for i in range(1, GDN_CONV_W):
        y = y + xp_ref[CONV_PAD - hist + i:CONV_PAD - hist + i + c, :] * convw_ref[i:i + 1, :]
    tail = x[c - hist:c, :]
    xp_ref[CONV_PAD - hist:CONV_PAD, :] = tail
    convo_ref[0] = tail
    xp_ref[tok, :] = _silu(y)

    beta_c = _sigmoid(bpre_ref[0])
    g_c = -jnp.exp(alog_ref[...]) * _softplus(apre_ref[0] + dtb_ref[...])
    g_r = -jnp.exp(alog_t_ref[...]) * _softplus(apre_t_ref[0] + dtb_t_ref[...])
    rows = lax.broadcasted_iota(jnp.int32, (c, c), 0)
    cols = lax.broadcasted_iota(jnp.int32, (c, c), 1)
    causal = rows >= cols
    strict = rows > cols
    gc_c = _dot_f32(causal.astype(F32), g_c)
    gc_r = _dot_f32(g_r, (rows <= cols).astype(F32))

    for kh in range(GDN_K_HEADS):
        qs = slice(kh * GDN_DK, (kh + 1) * GDN_DK)
        ks = slice(GDN_QK_DIM + kh * GDN_DK, GDN_QK_DIM + (kh + 1) * GDN_DK)
        q = xp_ref[tok, qs]
        k = xp_ref[tok, ks]
        q = q * lax.rsqrt(jnp.sum(q * q, axis=-1, keepdims=True) + RMS_EPS) * (GDN_DK ** -0.5)
        k = k * lax.rsqrt(jnp.sum(k * k, axis=-1, keepdims=True) + RMS_EPS)
        xp_ref[tok, qs] = q
        xp_ref[tok, ks] = k
        kq_ref[kh] = _dot_nt(jnp.concatenate([k, q], axis=0), k)

    diag8 = strict & ((rows >> 3) == (cols >> 3))
    for h in heads:
        kh = h // rep
        q = xp_ref[tok, kh * GDN_DK:(kh + 1) * GDN_DK]
        k = xp_ref[tok, GDN_QK_DIM + kh * GDN_DK:GDN_QK_DIM + (kh + 1) * GDN_DK]
        v = xp_ref[tok, 2 * GDN_QK_DIM + h * GDN_DV:2 * GDN_QK_DIM + (h + 1) * GDN_DV]
        gcol = gc_c[:, h:h + 1]
        bcol = beta_c[:, h:h + 1]
        diff = gcol - gc_r[h:h + 1, :]
        egc = jnp.exp(gcol)
        l = bcol * kq_ref[kh, 0:c, :] * jnp.exp(jnp.where(strict, diff, NEG_BIG))
        l_ref[h] = l
        a_ref[h] = kq_ref[kh, c:2 * c, :] * jnp.exp(jnp.where(causal, diff, NEG_BIG))
        rhs_ref[h, :, 0:GDN_DV] = v * bcol
        rhs_ref[h, :, GDN_DV:GDN_DV + GDN_DK] = k * (bcol * egc)
        lhs_ref[h, c:2 * c, :] = q * egc
        kd_ref[h] = k * jnp.exp(gcol[c - 1:c, :] - gcol)
        l8 = jnp.where(diag8, l, 0.0)
        e_ref[h] = -l8
        p_ref[h] = _dot(l8, l8)
    for h in heads:
        e = e_ref[h]
        p = p_ref[h]
        e_ref[h] = e + p + _dot(e, p)
        p_ref[h] = _dot(p, p)
    for h in heads:
        e = e_ref[h]
        p = p_ref[h]
        e_ref[h] = e + p + _dot(e, p)
    shift = 3
    while (2 << shift) <= c:
        lower_left = (((rows >> (shift + 1)) == (cols >> (shift + 1)))
                      & (((rows >> shift) & 1) == 1) & (((cols >> shift) & 1) == 0))
        for h in heads:
            n = jnp.where(lower_left, l_ref[h], 0.0)
            p_ref[h] = n + _dot(e_ref[h], n)
        for h in heads:
            t = p_ref[h]
            e = e_ref[h]
            e_ref[h] = e - t - _dot(t, e)
        shift += 1

    for h in heads:
        r = rhs_ref[h]
        uw = r + _dot_split(e_ref[h], r)
        rhs_ref[h, :, 0:GDN_DV] = uw[:, 0:GDN_DV]
        lhs_ref[h, 0:c, :] = uw[:, GDN_DV:GDN_DV + GDN_DK]
    for h in heads:
        ws = _dot(lhs_ref[h], s_ref[h])
        vn_ref[h] = rhs_ref[h, :, 0:GDN_DV] - ws[0:c]
        lhs_ref[h, c:2 * c, :] = ws[c:2 * c]
    for h in heads:
        v_new = vn_ref[h]
        o = lhs_ref[h, c:2 * c, :] + _dot(a_ref[h], v_new)
        s_ref[h] = s_ref[h] * jnp.exp(gc_c[c - 1:c, h:h + 1]) + _dot_tn(kd_ref[h], v_new)
        o = o * lax.rsqrt(jnp.mean(o * o, axis=-1, keepdims=True) + RMS_EPS) * normw_ref[...]
        zh = z_ref[0, :, h * GDN_DV:(h + 1) * GDN_DV]
        o_ref[0, :, h * GDN_DV:(h + 1) * GDN_DV] = (o * _silu(zh)).astype(o_ref.dtype)

    @pl.when(step == pl.num_programs(1) - 1)
    def _():
        so_ref[0] = s_ref[...]


def _alias_args(prev, first_input, first_output):
    if prev is None:
        return [], [], {}
    specs = [pl.BlockSpec(memory_space=pl.ANY)] * len(prev)
    aliases = {first_input + i: first_output + i for i in range(len(prev))}
    return list(prev), specs, aliases


def gdn_group(proj, gates, gates_t, conv0, s0, in_slot, conv_w, a_log, dt_bias, norm_w, *, row0, nb, t, c,
              slot, n_slots, prev):
    tokens = proj.shape[0]
    nchunk = t // c
    blk0 = row0 // c
    hv = GDN_V_HEADS
    proj3 = proj.reshape(tokens // c, c, proj.shape[1])
    bpre = gates[:, :hv].reshape(tokens // c, c, hv)
    apre = gates[:, hv:].reshape(tokens // c, c, hv)
    bpre_t = gates_t[:hv].reshape(hv, tokens // c, c).transpose(1, 0, 2)
    apre_t = gates_t[hv:].reshape(hv, tokens // c, c).transpose(1, 0, 2)
    z_blk = GDN_CONV_DIM // GDN_V_DIM

    def rows(b, s):
        return blk0 + b * nchunk + s

    kernel = functools.partial(_gdn_kernel, c=c)
    prev_args, prev_specs, aliases = _alias_args(prev, first_input=14, first_output=1)
    o, conv, s = pl.pallas_call(
        kernel,
        grid=(nb, nchunk),
        input_output_aliases=aliases,
        in_specs=[
            pl.BlockSpec((1, c, GDN_CONV_DIM), lambda b, s: (rows(b, s), 0, 0)),
            pl.BlockSpec((1, c, GDN_V_DIM), lambda b, s: (rows(b, s), 0, z_blk)),
            pl.BlockSpec((1, c, hv), lambda b, s: (rows(b, s), 0, 0)),
            pl.BlockSpec((1, c, hv), lambda b, s: (rows(b, s), 0, 0)),
            pl.BlockSpec((1, hv, c), lambda b, s: (rows(b, s), 0, 0)),
            pl.BlockSpec((1, hv, c), lambda b, s: (rows(b, s), 0, 0)),
            pl.BlockSpec((None, 1, GDN_CONV_W - 1, GDN_CONV_DIM), lambda b, s: (in_slot, b, 0, 0)),
            pl.BlockSpec((None, 1, hv, GDN_DK, GDN_DV), lambda b, s: (in_slot, b, 0, 0, 0)),
            pl.BlockSpec((GDN_CONV_W, GDN_CONV_DIM), lambda b, s: (0, 0)),
            pl.BlockSpec((1, hv), lambda b, s: (0, 0)),
            pl.BlockSpec((1, hv), lambda b, s: (0, 0)),
            pl.BlockSpec((hv, 1), lambda b, s: (0, 0)),
            pl.BlockSpec((hv, 1), lambda b, s: (0, 0)),
            pl.BlockSpec((1, GDN_DV), lambda b, s: (0, 0)),
        ] + prev_specs,
        out_specs=[
            pl.BlockSpec((1, c, GDN_V_DIM), lambda b, s: (b * nchunk + s, 0, 0)),
            pl.BlockSpec((None, 1, GDN_CONV_W - 1, GDN_CONV_DIM), lambda b, s: (slot, b, 0, 0)),
            pl.BlockSpec((None, 1, hv, GDN_DK, GDN_DV), lambda b, s: (slot, b, 0, 0, 0)),
        ],
        out_shape=[
            jax.ShapeDtypeStruct((nb * nchunk, c, GDN_V_DIM), BF16),
            jax.ShapeDtypeStruct((n_slots, nb, GDN_CONV_W - 1, GDN_CONV_DIM), F32),
            jax.ShapeDtypeStruct((n_slots, nb, hv, GDN_DK, GDN_DV), F32),
        ],
        scratch_shapes=[pltpu.VMEM((CONV_PAD + c, GDN_CONV_DIM), F32),
                        pltpu.VMEM((hv, GDN_DK, GDN_DV), F32),
                        pltpu.VMEM((GDN_K_HEADS, 2 * c, c), F32),
                        pltpu.VMEM((hv, c, c), F32),
                        pltpu.VMEM((hv, c, c), F32),
                        pltpu.VMEM((hv, c, c), F32),
                        pltpu.VMEM((hv, c, c), F32),
                        pltpu.VMEM((hv, c, GDN_DV + GDN_DK), F32),
                        pltpu.VMEM((hv, 2 * c, GDN_DK), F32),
                        pltpu.VMEM((hv, c, GDN_DK), F32),
                        pltpu.VMEM((hv, c, GDN_DV), F32)],
        compiler_params=_cparams("parallel", "arbitrary"),
        name=f"gdn_chunk{c}",
    )(proj3, proj3, bpre, apre, bpre_t, apre_t, conv0, s0, conv_w,
      a_log.reshape(1, hv), dt_bias.reshape(1, hv), a_log.reshape(hv, 1), dt_bias.reshape(hv, 1),
      norm_w.reshape(1, GDN_DV), *prev_args)
    return o.reshape(nb * t, GDN_V_DIM), (conv, s)


def _mlstm_kernel(main_ref, ig_ref, fg_ref, ig_t_ref, fg_t_ref, c0_ref, n0_ref, m0_ref,
                  bi_ref, bf_ref, bi_t_ref, bf_t_ref, normw_ref, *rest, c):
    o_ref, co_ref, no_ref, mo_ref, c_ref, n_ref, m_ref, qk_ref, qc_ref = rest[-9:]
    step = pl.program_id(1)

    @pl.when(step == 0)
    def _():
        c_ref[...] = c0_ref[0]
        n_ref[...] = n0_ref[0]
        m_ref[...] = jnp.broadcast_to(m0_ref[0], m_ref.shape)

    ig_c = ig_ref[0] + bi_ref[...]
    lf_c = _log_sigmoid(fg_ref[0] + bf_ref[...])
    ig_r = ig_t_ref[0] + bi_t_ref[...]
    lf_r = _log_sigmoid(fg_t_ref[0] + bf_t_ref[...])
    causal = _lower(c, strict=False)
    bc_c = _dot_f32(causal.astype(F32), lf_c)
    bc_r = _dot_f32(lf_r, _upper(c).astype(F32))

    for h in range(ML_HEADS):
        q = main_ref[0, :, h * ML_DQK:(h + 1) * ML_DQK]
        k = main_ref[0, :, ML_QK_DIM + h * ML_DQK:ML_QK_DIM + (h + 1) * ML_DQK] * (ML_DQK ** -0.5)
        qk_ref[h] = _dot_nt(q, k)
        qc_ref[h] = _dot(q, c_ref[h])

    for h in range(ML_HEADS):
        q = main_ref[0, :, h * ML_DQK:(h + 1) * ML_DQK]
        k = main_ref[0, :, ML_QK_DIM + h * ML_DQK:ML_QK_DIM + (h + 1) * ML_DQK] * (ML_DQK ** -0.5)
        v = main_ref[0, :, 2 * ML_QK_DIM + h * ML_DV:2 * ML_QK_DIM + (h + 1) * ML_DV]
        o_pre = main_ref[0, :, 2 * ML_QK_DIM + ML_V_DIM + h * ML_DV:2 * ML_QK_DIM + ML_V_DIM + (h + 1) * ML_DV]
        bcol = bc_c[:, h:h + 1]
        brow = bc_r[h:h + 1, :]
        icol = ig_c[:, h:h + 1]
        irow = ig_r[h:h + 1, :]
        m_prev = m_ref[h:h + 1, 0:1]
        cm = c_ref[h]
        nv = n_ref[h:h + 1, :]

        d = jnp.where(causal, bcol - brow + irow, NEG_BIG)
        inter = bcol + m_prev
        m_t = jnp.maximum(inter, jnp.max(d, axis=-1, keepdims=True))
        w_intra = jnp.exp(d - m_t)
        w_inter = jnp.exp(inter - m_t)
        sqk = qk_ref[h] * w_intra
        num = w_inter * qc_ref[h] + _dot(sqk, v)
        den = w_inter * jnp.sum(q * nv, axis=-1, keepdims=True) + jnp.sum(sqk, axis=-1, keepdims=True)
        hid = num / jnp.maximum(jnp.abs(den), jnp.exp(-m_t))

        b_last = bcol[c - 1:c, :]
        d_end = b_last - bcol + icol
        m_new = jnp.maximum(b_last + m_prev, jnp.max(d_end, axis=0, keepdims=True))
        wk = jnp.exp(d_end - m_new) * k
        scale = jnp.exp(b_last + m_prev - m_new)
        c_ref[h] = scale * cm + _dot_tn(wk, v)
        n_ref[h:h + 1, :] = scale * nv + jnp.sum(wk, axis=0, keepdims=True)
        m_ref[h:h + 1, :] = jnp.broadcast_to(m_new, (1, m_ref.shape[1]))

        hid = hid * lax.rsqrt(jnp.mean(hid * hid, axis=-1, keepdims=True) + RMS_EPS) * normw_ref[...]
        o_ref[0, :, h * ML_DV:(h + 1) * ML_DV] = (hid * _sigmoid(o_pre)).astype(o_ref.dtype)

    @pl.when(step == pl.num_programs(1) - 1)
    def _():
        co_ref[0] = c_ref[...]
        no_ref[0] = n_ref[...]
        mo_ref[0] = m_ref[:, 0:1]


def mlstm_group(proj, gates, gates_t, c0, n0, m0, in_slot, gate_b, norm_w, *, row0, nb, t, c,
                slot, n_slots, prev):
    tokens = proj.shape[0]
    nchunk = t // c
    blk0 = row0 // c
    nh = ML_HEADS
    proj3 = proj.reshape(tokens // c, c, proj.shape[1])
    ig = gates[:, :nh].reshape(tokens // c, c, nh)
    fg = gates[:, nh:].reshape(tokens // c, c, nh)
    ig_t = gates_t[:nh].reshape(nh, tokens // c, c).transpose(1, 0, 2)
    fg_t = gates_t[nh:].reshape(nh, tokens // c, c).transpose(1, 0, 2)

    def rows(b, s):
        return blk0 + b * nchunk + s

    kernel = functools.partial(_mlstm_kernel, c=c)
    prev_args, prev_specs, aliases = _alias_args(prev, first_input=13, first_output=1)
    o, cm, nv, m = pl.pallas_call(
        kernel,
        grid=(nb, nchunk),
        input_output_aliases=aliases,
        in_specs=[
            pl.BlockSpec((1, c, ML_MAIN_DIM), lambda b, s: (rows(b, s), 0, 0)),
            pl.BlockSpec((1, c, nh), lambda b, s: (rows(b, s), 0, 0)),
            pl.BlockSpec((1, c, nh), lambda b, s: (rows(b, s), 0, 0)),
            pl.BlockSpec((1, nh, c), lambda b, s: (rows(b, s), 0, 0)),
            pl.BlockSpec((1, nh, c), lambda b, s: (rows(b, s), 0, 0)),
            pl.BlockSpec((None, 1, nh, ML_DQK, ML_DV), lambda b, s: (in_slot, b, 0, 0, 0)),
            pl.BlockSpec((None, 1, nh, ML_DQK), lambda b, s: (in_slot, b, 0, 0)),
            pl.BlockSpec((None, 1, nh, 1), lambda b, s: (in_slot, b, 0, 0)),
            pl.BlockSpec((1, nh), lambda b, s: (0, 0)),
            pl.BlockSpec((1, nh), lambda b, s: (0, 0)),
            pl.BlockSpec((nh, 1), lambda b, s: (0, 0)),
            pl.BlockSpec((nh, 1), lambda b, s: (0, 0)),
            pl.BlockSpec((1, ML_DV), lambda b, s: (0, 0)),
        ] + prev_specs,
        out_specs=[
            pl.BlockSpec((1, c, ML_V_DIM), lambda b, s: (b * nchunk + s, 0, 0)),
            pl.BlockSpec((None, 1, nh, ML_DQK, ML_DV), lambda b, s: (slot, b, 0, 0, 0)),
            pl.BlockSpec((None, 1, nh, ML_DQK), lambda b, s: (slot, b, 0, 0)),
            pl.BlockSpec((None, 1, nh, 1), lambda b, s: (slot, b, 0, 0)),
        ],
        out_shape=[
            jax.ShapeDtypeStruct((nb * nchunk, c, ML_V_DIM), BF16),
            jax.ShapeDtypeStruct((n_slots, nb, nh, ML_DQK, ML_DV), F32),
            jax.ShapeDtypeStruct((n_slots, nb, nh, ML_DQK), F32),
            jax.ShapeDtypeStruct((n_slots, nb, nh, 1), F32),
        ],
        scratch_shapes=[pltpu.VMEM((nh, ML_DQK, ML_DV), F32),
                        pltpu.VMEM((nh, ML_DQK), F32),
                        pltpu.VMEM((nh, 128), F32),
                        pltpu.VMEM((nh, c, c), F32),
                        pltpu.VMEM((nh, c, ML_DV), F32)],
        compiler_params=_cparams("parallel", "arbitrary"),
        name=f"mlstm_chunk{c}",
    )(proj3, ig, fg, ig_t, fg_t, c0, n0, m0.reshape(m0.shape + (1,)),
      gate_b[:nh].reshape(1, nh), gate_b[nh:].reshape(1, nh),
      gate_b[:nh].reshape(nh, 1), gate_b[nh:].reshape(nh, 1), norm_w.reshape(1, ML_DV), *prev_args)
    return o.reshape(nb * t, ML_V_DIM), (cm, nv, m)


_CAND_PAIRS = [(a, b) for a in range(PEER_TOPK) for b in range(PEER_TOPK) if (a + 1) * (b + 1) <= PEER_TOPK]
_CAND_ROWS = -(-len(_CAND_PAIRS) // 8) * 8


def _top_values(work, count):
    rows = lax.broadcasted_iota(jnp.int32, work.shape, 0)
    out = []
    for r in range(count):
        m = jnp.max(work, axis=0, keepdims=True)
        out.append(m)
        if r + 1 < count:
            first = jnp.min(jnp.where(work == m, rows, work.shape[0]), axis=0, keepdims=True)
            work = jnp.where(rows == first, NEG_INF, work)
    return out


def _sorting_network(n):
    pairs = []
    p = 1
    while p < n:
        k = p
        while k >= 1:
            for j in range(k % p, n - k, 2 * k):
                for i in range(min(k, n - j - k)):
                    if (i + j) // (2 * p) == (i + j + k) // (2 * p):
                        pairs.append((i + j, i + j + k))
            k //= 2
        p *= 2
    return pairs


def _compare_exchange(vals, i, j):
    vals[i], vals[j] = jnp.maximum(vals[i], vals[j]), jnp.minimum(vals[i], vals[j])


def _top16_of_128(s):
    k = PEER_TOPK
    slab = s.shape[0] // k
    vals = [s[i * slab:(i + 1) * slab, :] for i in range(k)]
    for i, j in _sorting_network(k):
        _compare_exchange(vals, i, j)
    shift = slab // 2
    while shift >= 1:
        vals = [jnp.maximum(vals[i], pltpu.roll(vals[k - 1 - i], shift, 0)) for i in range(k)]
        stride = k // 2
        while stride >= 1:
            for i in range(k):
                if (i // stride) % 2 == 0:
                    _compare_exchange(vals, i, i + stride)
            stride //= 2
        shift //= 2
    return [v[0:1, :] for v in vals]


def _peer_route_kernel(q_ref, keys_ref, th1_ref, w1_ref, s2_ref, w2_ref, cand_ref):
    cand_ref[...] = jnp.full(cand_ref.shape, NEG_INF, F32)
    for h in range(PEER_HEADS):
        scores, tops = [], []
        for p in range(2):
            col = (2 * h + p) * PEER_HALF
            s = _dot_nt(keys_ref[h, p], q_ref[:, col:col + PEER_HALF])
            scores.append(s)
            tops.append(_top16_of_128(s))
        for i, (a, b) in enumerate(_CAND_PAIRS):
            cand_ref[i:i + 1, :] = tops[0][a] + tops[1][b]
        cand = cand_ref[...]
        tau = _top_values(cand, PEER_TOPK)[-1]
        max1, max2 = tops[0][0], tops[1][0]
        z = jnp.sum(jnp.where(cand >= tau, jnp.exp(cand - (max1 + max2)), 0.0), axis=0, keepdims=True)
        cut = jnp.full(scores[0].shape, POS_INF, F32)
        for a in range(PEER_TOPK):
            cut_a = jnp.full(tau.shape, POS_INF, F32)
            for b in range(PEER_TOPK // (a + 1)):
                cut_a = jnp.where(tops[0][a] + tops[1][b] >= tau, tops[1][b], cut_a)
            cut = jnp.where(scores[0] == tops[0][a], cut_a, cut)
        th1_ref[h] = cut
        w1_ref[h] = jnp.exp(scores[0] - max1)
        s2_ref[h] = scores[1]
        w2_ref[h] = jnp.exp(scores[1] - max2) / z


def peer_route(q, keys, *, tm):
    m = q.shape[0]
    nh = PEER_HEADS
    table = jax.ShapeDtypeStruct((nh, PEER_N_KEYS, m), F32)
    table_spec = pl.BlockSpec((nh, PEER_N_KEYS, tm), lambda i: (0, 0, i))
    return pl.pallas_call(
        _peer_route_kernel,
        grid=(m // tm,),
        in_specs=[pl.BlockSpec((tm, q.shape[1]), lambda i: (i, 0)),
                  pl.BlockSpec(keys.shape, lambda i: (0, 0, 0, 0))],
        out_specs=[table_spec] * 4,
        out_shape=[table] * 4,
        scratch_shapes=[pltpu.VMEM((_CAND_ROWS, tm), F32)],
        compiler_params=_cparams("parallel"),
        name="peer_route",
    )(q, keys)


def _table_cast_kernel(u_ref, v_ref, ub_ref, vt_ref):
    ub_ref[...] = u_ref[...].astype(BF16)
    vt_ref[...] = v_ref[...].T.astype(BF16)


def cast_tables(u_stack, v_stack, layer, *, rows):
    ne, d = u_stack.shape[1:]
    return pl.pallas_call(
        _table_cast_kernel,
        grid=(ne // rows,),
        in_specs=[pl.BlockSpec((None, rows, d), lambda i: (layer, i, 0)),
                  pl.BlockSpec((None, rows, d), lambda i: (layer, i, 0))],
        out_specs=[pl.BlockSpec((rows, d), lambda i: (i, 0)),
                   pl.BlockSpec((d, rows), lambda i: (0, i))],
        out_shape=[jax.ShapeDtypeStruct((ne, d), BF16), jax.ShapeDtypeStruct((d, ne), BF16)],
        compiler_params=_cparams("parallel"),
        name="peer_table_cast",
    )(u_stack, v_stack)


LANES = 128


def _peer_main_kernel(xb_ref, u_ref, vt_ref, th1_ref, w1_ref, s2_ref, w2_ref, x_ref, g_ref, b_ref,
                      o_ref, ob_ref, acc_ref, ht_ref, a_ref, *, te):
    e = pl.program_id(1)
    groups = te // PEER_N_KEYS
    tm = ht_ref.shape[1]

    @pl.when(e == 0)
    def _():
        acc_ref[...] = jnp.zeros_like(acc_ref)

    ht_ref[...] = _dot_nt(u_ref[...], xb_ref[...])
    for cc in range(groups):
        key1 = e * groups + cc
        rows = slice(cc * PEER_N_KEYS, (cc + 1) * PEER_N_KEYS)
        for t0 in range(0, tm, 2 * LANES):
            pair = slice(t0, t0 + 2 * LANES)
            th_rows = [th1_ref[h, pl.ds(key1, 1), pair] for h in range(PEER_HEADS)]
            w1_rows = [w1_ref[h, pl.ds(key1, 1), pair] for h in range(PEER_HEADS)]
            for half in range(2):
                lanes = slice(t0 + half * LANES, t0 + (half + 1) * LANES)
                sub = slice(half * LANES, (half + 1) * LANES)
                gate = jnp.zeros((PEER_N_KEYS, LANES), F32)
                for h in range(PEER_HEADS):
                    hit = s2_ref[h, :, lanes] >= th_rows[h][:, sub]
                    gate = gate + jnp.where(hit, w2_ref[h, :, lanes], 0.0) * w1_rows[h][:, sub]
                a_ref[rows, lanes] = (_gelu_tanh(ht_ref[rows, lanes]) * gate).astype(BF16)
    acc_ref[...] += _dot(vt_ref[...], a_ref[...])

    @pl.when(e == pl.num_programs(1) - 1)
    def _():
        for t0 in range(0, tm, LANES):
            tok = slice(t0, t0 + LANES)
            y = _layer_norm_rows(DEEPNORM_ALPHA * x_ref[tok, :] + acc_ref[:, tok].T, g_ref[...], b_ref[...])
            o_ref[tok, :] = y
            ob_ref[tok, :] = y.astype(BF16)


def peer_main(xb, x, u, vt, th1, w1, s2, w2, g, b, *, tm, te):
    m, d = x.shape
    ne = u.shape[0]
    nh = PEER_HEADS
    kernel = functools.partial(_peer_main_kernel, te=te)
    once = pl.Buffered(1)
    table_spec = pl.BlockSpec((nh, PEER_N_KEYS, tm), lambda i, e: (0, 0, i), pipeline_mode=once)
    return pl.pallas_call(
        kernel,
        grid=(m // tm, ne // te),
        in_specs=[pl.BlockSpec((tm, d), lambda i, e: (i, 0), pipeline_mode=once),
                  pl.BlockSpec((te, d), lambda i, e: (e, 0)),
                  pl.BlockSpec((d, te), lambda i, e: (0, e)),
                  table_spec, table_spec, table_spec, table_spec,
                  pl.BlockSpec((tm, d), lambda i, e: (i, 0), pipeline_mode=once),
                  pl.BlockSpec((1, d), lambda i, e: (0, 0)),
                  pl.BlockSpec((1, d), lambda i, e: (0, 0))],
        out_specs=[pl.BlockSpec((tm, d), lambda i, e: (i, 0), pipeline_mode=once),
                   pl.BlockSpec((tm, d), lambda i, e: (i, 0), pipeline_mode=once)],
        out_shape=[jax.ShapeDtypeStruct((m, d), F32), jax.ShapeDtypeStruct((m, d), BF16)],
        scratch_shapes=[pltpu.VMEM((d, tm), F32), pltpu.VMEM((te, tm), F32), pltpu.VMEM((te, tm), BF16)],
        compiler_params=_cparams("parallel", "arbitrary"),
        name="peer_main",
    )(xb, u, vt, th1, w1, s2, w2, x, g.reshape(1, d), b.reshape(1, d))


TOKEN_TILE = 3072
GATE_TOKEN_TILE = 1024
PROJ_COL_TILE = 512
OUT_K_TILE = 512
OUT_ROW_TILE = 1024
PEER_TOKEN_TILE = 512
PEER_EXPERT_TILE = 1024
TABLE_CAST_ROWS = 512


def kernel(x_prompt, x_sample, state_gdn_conv, state_gdn_s, state_mlstm_c, state_mlstm_n, state_mlstm_m,
           gdn_w_in, gdn_conv_w, gdn_a_log, gdn_dt_bias, gdn_norm_w, gdn_w_out,
           ml_w_in, ml_gate_b, ml_norm_w, ml_w_out,
           ln_mix_g, ln_mix_b, ln_ffn_g, ln_ffn_b,
           peer_w_q, peer_keys, peer_u, peer_v):
    bp, tp, d = x_prompt.shape
    bs, ts, _ = x_sample.shape
    np_tok = bp * tp
    ns_tok = bs * ts
    x = jnp.concatenate([x_prompt.reshape(np_tok, d), x_sample.reshape(ns_tok, d)], axis=0)
    xb = x.astype(BF16)
    cp = min(GDN_CHUNK, tp)
    cs = min(GDN_CHUNK, ts)

    n_gdn = state_gdn_s.shape[0]
    n_ml = state_mlstm_c.shape[0]
    zero_gdn = (jnp.zeros((1, bp) + state_gdn_conv.shape[2:], F32), jnp.zeros((1, bp) + state_gdn_s.shape[2:], F32))
    zero_ml = (jnp.zeros((1, bp) + state_mlstm_c.shape[2:], F32), jnp.zeros((1, bp) + state_mlstm_n.shape[2:], F32),
               jnp.zeros((1, bp) + state_mlstm_m.shape[2:], F32))
    p_gdn = s_gdn = p_ml = s_ml = None
    for layer in range(DEPTH):
        j = layer // 2
        prompt = dict(row0=0, nb=bp, t=tp, c=cp, slot=j)
        sample = dict(row0=np_tok, nb=bs, t=ts, c=cs, slot=j)
        if layer % 2 == 0:
            proj = matmul(xb, gdn_w_in, j, GDN_MAIN_DIM, tm=TOKEN_TILE, tn=PROJ_COL_TILE)
            gates, gates_t = gate_proj(xb, gdn_w_in[j, :, GDN_MAIN_DIM:], tm=GATE_TOKEN_TILE)
            shared = (gdn_conv_w[j], gdn_a_log[j], gdn_dt_bias[j], gdn_norm_w[j])
            o_p, p_gdn = gdn_group(proj, gates, gates_t, *zero_gdn, 0, *shared, n_slots=n_gdn, prev=p_gdn, **prompt)
            o_s, s_gdn = gdn_group(proj, gates, gates_t, state_gdn_conv, state_gdn_s, j, *shared,
                                   n_slots=n_gdn, prev=s_gdn, **sample)
            w_out = gdn_w_out
        else:
            proj = matmul(xb, ml_w_in, j, ML_MAIN_DIM, tm=TOKEN_TILE, tn=PROJ_COL_TILE)
            gates, gates_t = gate_proj(xb, ml_w_in[j, :, ML_MAIN_DIM:], tm=GATE_TOKEN_TILE)
            shared = (ml_gate_b[j], ml_norm_w[j])
            o_p, p_ml = mlstm_group(proj, gates, gates_t, *zero_ml, 0, *shared, n_slots=n_ml, prev=p_ml, **prompt)
            o_s, s_ml = mlstm_group(proj, gates, gates_t, state_mlstm_c, state_mlstm_n, state_mlstm_m, j, *shared,
                                    n_slots=n_ml, prev=s_ml, **sample)
            w_out = ml_w_out
        x, xb = out_proj_ln(o_p, o_s, w_out, j, x, ln_mix_g[layer], ln_mix_b[layer],
                            tm=OUT_ROW_TILE, tk=OUT_K_TILE)

        q = matmul(xb, peer_w_q, layer, peer_w_q.shape[2], tm=TOKEN_TILE, tn=PROJ_COL_TILE)
        tables = peer_route(q, peer_keys[layer], tm=PEER_TOKEN_TILE)
        u_bf16, vt_bf16 = cast_tables(peer_u, peer_v, layer, rows=TABLE_CAST_ROWS)
        x, xb = peer_main(xb, x, u_bf16, vt_bf16, *tables,
                          ln_ffn_g[layer], ln_ffn_b[layer], tm=PEER_TOKEN_TILE, te=PEER_EXPERT_TILE)

    y_prompt = x[:np_tok].reshape(bp, tp, d)
    y_sample = x[np_tok:].reshape(bs, ts, d)
    return (y_prompt, y_sample,
            p_gdn[0], p_gdn[1], p_ml[0], p_ml[1], p_ml[2].reshape(n_ml, bp, ML_HEADS),
            s_gdn[0], s_gdn[1], s_ml[0], s_ml[1], s_ml[2].reshape(n_ml, bs, ML_HEADS))
```

```python
import functools

import jax
import jax.numpy as jnp
from jax import lax
from jax.experimental import pallas as pl
from jax.experimental.pallas import tpu as pltpu

F32 = jnp.float32
BF16 = jnp.bfloat16

D_MODEL = 2048
DEPTH = 4
GDN_K_HEADS = 16
GDN_V_HEADS = 32
GDN_DK = 128
GDN_DV = 128
GDN_QK_DIM = GDN_K_HEADS * GDN_DK
GDN_V_DIM = GDN_V_HEADS * GDN_DV
GDN_CONV_DIM = 2 * GDN_QK_DIM + GDN_V_DIM
GDN_MAIN_DIM = GDN_CONV_DIM + GDN_V_DIM
GDN_CONV_W = 4
GDN_CHUNK = 64
ML_HEADS = 8
ML_DQK = 128
ML_DV = 256
ML_QK_DIM = ML_HEADS * ML_DQK
ML_V_DIM = ML_HEADS * ML_DV
ML_MAIN_DIM = 2 * ML_QK_DIM + 2 * ML_V_DIM
ML_CHUNK = 64
PEER_HEADS = 8
PEER_N_KEYS = 128
PEER_HALF = 128
PEER_TOPK = 16
DEEPNORM_ALPHA = (2 * DEPTH) ** 0.25
LN_EPS = 1e-5
RMS_EPS = 1e-6
NEG_BIG = -1e30
NEG_INF = float("-inf")
POS_INF = float("inf")

VMEM_LIMIT_BYTES = 56 * 1024 * 1024
LANE_TILE = 128
HIGHEST = lax.Precision.HIGHEST


def _cparams(*sem):
    return pltpu.CompilerParams(dimension_semantics=sem, vmem_limit_bytes=VMEM_LIMIT_BYTES)


def _dot(a, b):
    return jnp.dot(a.astype(BF16), b.astype(BF16), preferred_element_type=F32)


def _dot_nt(a, b):
    return lax.dot_general(a.astype(BF16), b.astype(BF16), (((1,), (1,)), ((), ())),
                           preferred_element_type=F32)


def _dot_tn(a, b):
    return lax.dot_general(a.astype(BF16), b.astype(BF16), (((0,), (0,)), ((), ())),
                           preferred_element_type=F32)


def _dot_f32(a, b):
    return jnp.dot(a, b, preferred_element_type=F32, precision=HIGHEST)


def _sigmoid(x):
    return 1.0 / (1.0 + jnp.exp(-x))


def _silu(x):
    return x * _sigmoid(x)


def _softplus(x):
    return jnp.maximum(x, 0.0) + jnp.log(1.0 + jnp.exp(-jnp.abs(x)))


def _log_sigmoid(x):
    return -_softplus(-x)


def _gelu_tanh(x):
    return 0.5 * x * (1.0 + jnp.tanh(0.7978845608028654 * (x + 0.044715 * (x * x * x))))


def _lower(c, strict):
    r = lax.broadcasted_iota(jnp.int32, (c, c), 0)
    k = lax.broadcasted_iota(jnp.int32, (c, c), 1)
    return (r > k) if strict else (r >= k)


def _upper(c):
    r = lax.broadcasted_iota(jnp.int32, (c, c), 0)
    k = lax.broadcasted_iota(jnp.int32, (c, c), 1)
    return r <= k


def _layer_norm_rows(v, g, b):
    mu = jnp.mean(v, axis=-1, keepdims=True)
    d = v - mu
    var = jnp.mean(d * d, axis=-1, keepdims=True)
    return d * lax.rsqrt(var + LN_EPS) * g + b


def _matmul_kernel(x_ref, w_ref, o_ref):
    o_ref[...] = _dot(x_ref[...], w_ref[...])


def matmul(x, w_stack, layer, n_cols, *, tm, tn):
    m, k = x.shape
    return pl.pallas_call(
        _matmul_kernel,
        grid=(m // tm, n_cols // tn),
        in_specs=[pl.BlockSpec((tm, k), lambda i, j: (i, 0), pipeline_mode=pl.Buffered(1)),
                  pl.BlockSpec((None, k, tn), lambda i, j: (layer, 0, j))],
        out_specs=pl.BlockSpec((tm, tn), lambda i, j: (i, j)),
        out_shape=jax.ShapeDtypeStruct((m, n_cols), F32),
        compiler_params=_cparams("parallel", "arbitrary"),
        name="proj_matmul",
    )(x, w_stack)


def _gate_proj_kernel(x_ref, w_ref, wt_ref, o_ref, ot_ref):
    x = x_ref[...]
    o_ref[...] = _dot(x, w_ref[...])
    ot_ref[...] = _dot_nt(wt_ref[...], x)


def gate_proj(x, w_gate, *, tm):
    m, k = x.shape
    n = w_gate.shape[1]
    return pl.pallas_call(
        _gate_proj_kernel,
        grid=(m // tm,),
        in_specs=[pl.BlockSpec((tm, k), lambda i: (i, 0)),
                  pl.BlockSpec((k, n), lambda i: (0, 0)),
                  pl.BlockSpec((n, k), lambda i: (0, 0))],
        out_specs=[pl.BlockSpec((tm, n), lambda i: (i, 0)),
                   pl.BlockSpec((n, tm), lambda i: (0, i))],
        out_shape=[jax.ShapeDtypeStruct((m, n), F32), jax.ShapeDtypeStruct((n, m), F32)],
        compiler_params=_cparams("parallel"),
        name="gate_proj",
    )(x, w_gate, w_gate.T)


def _out_proj_ln_kernel(hp_ref, hs_ref, w_ref, x_ref, g_ref, b_ref, o_ref, ob_ref, acc_ref, *, prompt_tiles):
    i = pl.program_id(0)
    kk = pl.program_id(1)

    @pl.when(kk == 0)
    def _():
        acc_ref[...] = jnp.zeros_like(acc_ref)

    @pl.when(i < prompt_tiles)
    def _():
        acc_ref[...] += _dot(hp_ref[...], w_ref[...])

    @pl.when(i >= prompt_tiles)
    def _():
        acc_ref[...] += _dot(hs_ref[...], w_ref[...])

    @pl.when(kk == pl.num_programs(1) - 1)
    def _():
        y = _layer_norm_rows(DEEPNORM_ALPHA * x_ref[...] + acc_ref[...], g_ref[...], b_ref[...])
        o_ref[...] = y
        ob_ref[...] = y.astype(BF16)


def out_proj_ln(h_prompt, h_sample, w_stack, layer, x, g, b, *, tm, tk):
    mp, k = h_prompt.shape
    ms = h_sample.shape[0]
    d = w_stack.shape[2]
    prompt_tiles = mp // tm
    last_k = k // tk - 1
    kernel = functools.partial(_out_proj_ln_kernel, prompt_tiles=prompt_tiles)
    return pl.pallas_call(
        kernel,
        grid=((mp + ms) // tm, k // tk),
        in_specs=[pl.BlockSpec((tm, tk), lambda i, j: (jnp.minimum(i, prompt_tiles - 1),
                                                        jnp.where(i < prompt_tiles, j, last_k))),
                  pl.BlockSpec((tm, tk), lambda i, j: (jnp.maximum(i - prompt_tiles, 0),
                                                        jnp.where(i < prompt_tiles, 0, j))),
                  pl.BlockSpec((None, tk, d), lambda i, j: (layer, j, 0)),
                  pl.BlockSpec((tm, d), lambda i, j: (i, 0), pipeline_mode=pl.Buffered(1)),
                  pl.BlockSpec((1, d), lambda i, j: (0, 0)),
                  pl.BlockSpec((1, d), lambda i, j: (0, 0))],
        out_specs=[pl.BlockSpec((tm, d), lambda i, j: (i, 0), pipeline_mode=pl.Buffered(1)),
                   pl.BlockSpec((tm, d), lambda i, j: (i, 0), pipeline_mode=pl.Buffered(1))],
        out_shape=[jax.ShapeDtypeStruct((mp + ms, d), F32), jax.ShapeDtypeStruct((mp + ms, d), BF16)],
        scratch_shapes=[pltpu.VMEM((tm, d), F32)],
        compiler_params=_cparams("parallel", "arbitrary"),
        name="out_proj_ln",
    )(h_prompt, h_sample, w_stack, x, g.reshape(1, d), b.reshape(1, d))


CONV_PAD = 8


def _dot_split(e, r):
    c = e.shape[0]
    e_hi = e.astype(BF16).astype(F32)
    r_hi = r.astype(BF16).astype(F32)
    stacked = _dot(jnp.concatenate([e_hi, e - e_hi], axis=0), r_hi)
    return stacked[0:c] + stacked[c:2 * c] + _dot(e_hi, r - r_hi)


def _cast_table_slab(u_ref, v_ref, ub_ref, vt_ref):
    ub_ref[...] = u_ref[...].astype(BF16)
    vt_ref[...] = v_ref[...].T.astype(BF16)


def _cast_specs(tables, layer, steps, step_index):
    if tables is None:
        return [], [], [], []
    ne, d = tables[0].shape[1:]
    slab = ne // steps
    in_specs = [pl.BlockSpec((None, slab, d), lambda b, s: (layer, step_index(b, s), 0))] * 2
    out_specs = [pl.BlockSpec((slab, d), lambda b, s: (step_index(b, s), 0)),
                 pl.BlockSpec((d, slab), lambda b, s: (0, step_index(b, s)))]
    out_shape = [jax.ShapeDtypeStruct((ne, d), BF16), jax.ShapeDtypeStruct((d, ne), BF16)]
    return list(tables), in_specs, out_specs, out_shape


def _gdn_kernel(qkv_ref, z_ref, bpre_ref, apre_ref, bpre_t_ref, apre_t_ref, conv0_ref, s0_ref,
                convw_ref, alog_ref, dtb_ref, alog_t_ref, dtb_t_ref, normw_ref, *rest, c, cast):
    (xp_ref, s_ref, kq_ref, l_ref, a_ref, e_ref, p_ref, rhs_ref, lhs_ref, kd_ref, vn_ref) = rest[-11:]
    if cast:
        _cast_table_slab(rest[0], rest[1], rest[-13], rest[-12])
        o_ref, convo_ref, so_ref = rest[-16:-13]
    else:
        o_ref, convo_ref, so_ref = rest[-14:-11]
    step = pl.program_id(1)
    hist = GDN_CONV_W - 1
    heads = range(GDN_V_HEADS)
    rep = GDN_V_HEADS // GDN_K_HEADS
    tok = slice(CONV_PAD, CONV_PAD + c)

    @pl.when(step == 0)
    def _():
        xp_ref[CONV_PAD - hist:CONV_PAD, :] = conv0_ref[0]
        s_ref[...] = s0_ref[0]

    x = qkv_ref[0]
    xp_ref[tok, :] = x
    y = xp_ref[CONV_PAD - hist:CONV_PAD - hist + c, :] * convw_ref[0:1, :]
    for i in range(1, GDN_CONV_W):
        y = y + xp_ref[CONV_PAD - hist + i:CONV_PAD - hist + i + c, :] * convw_ref[i:i + 1, :]
    tail = x[c - hist:c, :]
    xp_ref[CONV_PAD - hist:CONV_PAD, :] = tail
    convo_ref[0] = tail
    xp_ref[tok, :] = _silu(y)

    beta_c = _sigmoid(bpre_ref[0])
    g_c = -jnp.exp(alog_ref[...]) * _softplus(apre_ref[0] + dtb_ref[...])
    g_r = -jnp.exp(alog_t_ref[...]) * _softplus(apre_t_ref[0] + dtb_t_ref[...])
    rows = lax.broadcasted_iota(jnp.int32, (c, c), 0)
    cols = lax.broadcasted_iota(jnp.int32, (c, c), 1)
    causal = rows >= cols
    strict = rows > cols
    gc_c = _dot_f32(causal.astype(F32), g_c)
    gc_r = _dot_f32(g_r, (rows <= cols).astype(F32))

    for kh in range(GDN_K_HEADS):
        qs = slice(kh * GDN_DK, (kh + 1) * GDN_DK)
        ks = slice(GDN_QK_DIM + kh * GDN_DK, GDN_QK_DIM + (kh + 1) * GDN_DK)
        q = xp_ref[tok, qs]
        k = xp_ref[tok, ks]
        q = q * lax.rsqrt(jnp.sum(q * q, axis=-1, keepdims=True) + RMS_EPS) * (GDN_DK ** -0.5)
        k = k * lax.rsqrt(jnp.sum(k * k, axis=-1, keepdims=True) + RMS_EPS)
        xp_ref[tok, qs] = q
        xp_ref[tok, ks] = k
        kq_ref[kh] = _dot_nt(jnp.concatenate([k, q], axis=0), k)

    diag8 = strict & ((rows >> 3) == (cols >> 3))
    for h in heads:
        kh = h // rep
        q = xp_ref[tok, kh * GDN_DK:(kh + 1) * GDN_DK]
        k = xp_ref[tok, GDN_QK_DIM + kh * GDN_DK:GDN_QK_DIM + (kh + 1) * GDN_DK]
        v = xp_ref[tok, 2 * GDN_QK_DIM + h * GDN_DV:2 * GDN_QK_DIM + (h + 1) * GDN_DV]
        gcol = gc_c[:, h:h + 1]
        bcol = beta_c[:, h:h + 1]
        diff = gcol - gc_r[h:h + 1, :]
        egc = jnp.exp(gcol)
        l = bcol * kq_ref[kh, 0:c, :] * jnp.exp(jnp.where(strict, diff, NEG_BIG))
        l_ref[h] = l
        a_ref[h] = kq_ref[kh, c:2 * c, :] * jnp.exp(jnp.where(causal, diff, NEG_BIG))
        rhs_ref[h, :, 0:GDN_DV] = v * bcol
        rhs_ref[h, :, GDN_DV:GDN_DV + GDN_DK] = k * (bcol * egc)
        lhs_ref[h, c:2 * c, :] = q * egc
        kd_ref[h] = k * jnp.exp(gcol[c - 1:c, :] - gcol)
        l8 = jnp.where(diag8, l, 0.0)
        e_ref[h] = -l8
        p_ref[h] = _dot(l8, l8)
    for h in heads:
        e = e_ref[h]
        p = p_ref[h]
        e_ref[h] = e + p + _dot(e, p)
        p_ref[h] = _dot(p, p)
    for h in heads:
        e = e_ref[h]
        p = p_ref[h]
        e_ref[h] = e + p + _dot(e, p)
    shift = 3
    while (2 << shift) <= c:
        lower_left = (((rows >> (shift + 1)) == (cols >> (shift + 1)))
                      & (((rows >> shift) & 1) == 1) & (((cols >> shift) & 1) == 0))
        for h in heads:
            n = jnp.where(lower_left, l_ref[h], 0.0)
            p_ref[h] = n + _dot(e_ref[h], n)
        for h in heads:
            t = p_ref[h]
            e = e_ref[h]
            e_ref[h] = e - t - _dot(t, e)
        shift += 1

    for h in heads:
        r = rhs_ref[h]
        uw = r + _dot_split(e_ref[h], r)
        rhs_ref[h, :, 0:GDN_DV] = uw[:, 0:GDN_DV]
        lhs_ref[h, 0:c, :] = uw[:, GDN_DV:GDN_DV + GDN_DK]
    for h in heads:
        ws = _dot(lhs_ref[h], s_ref[h])
        vn_ref[h] = rhs_ref[h, :, 0:GDN_DV] - ws[0:c]
        lhs_ref[h, c:2 * c, :] = ws[c:2 * c]
    for h in heads:
        v_new = vn_ref[h]
        o = lhs_ref[h, c:2 * c, :] + _dot(a_ref[h], v_new)
        s_ref[h] = s_ref[h] * jnp.exp(gc_c[c - 1:c, h:h + 1]) + _dot_tn(kd_ref[h], v_new)
        o = o * lax.rsqrt(jnp.mean(o * o, axis=-1, keepdims=True) + RMS_EPS) * normw_ref[...]
        zh = z_ref[0, :, h * GDN_DV:(h + 1) * GDN_DV]
        o_ref[0, :, h * GDN_DV:(h + 1) * GDN_DV] = (o * _silu(zh)).astype(o_ref.dtype)

    @pl.when(step == pl.num_programs(1) - 1)
    def _():
        so_ref[0] = s_ref[...]


def _alias_args(prev, first_input, first_output):
    if prev is None:
        return [], [], {}
    specs = [pl.BlockSpec(memory_space=pl.ANY)] * len(prev)
    aliases = {first_input + i: first_output + i for i in range(len(prev))}
    return list(prev), specs, aliases


def gdn_group(proj, gates, gates_t, conv0, s0, in_slot, conv_w, a_log, dt_bias, norm_w, *, row0, nb, t, c,
              slot, n_slots, prev, tables=None, table_layer=0):
    tokens = proj.shape[0]
    nchunk = t // c
    blk0 = row0 // c
    hv = GDN_V_HEADS
    proj3 = proj.reshape(tokens // c, c, proj.shape[1])
    bpre = gates[:, :hv].reshape(tokens // c, c, hv)
    apre = gates[:, hv:].reshape(tokens // c, c, hv)
    bpre_t = gates_t[:hv].reshape(hv, tokens // c, c).transpose(1, 0, 2)
    apre_t = gates_t[hv:].reshape(hv, tokens // c, c).transpose(1, 0, 2)
    z_blk = GDN_CONV_DIM // GDN_V_DIM

    def rows(b, s):
        return blk0 + b * nchunk + s

    kernel = functools.partial(_gdn_kernel, c=c, cast=tables is not None)
    cast_args, cast_in, cast_out, cast_shape = _cast_specs(tables, table_layer, nb * nchunk,
                                                           lambda b, s: b * nchunk + s)
    prev_args, prev_specs, aliases = _alias_args(prev, first_input=14 + len(cast_args), first_output=1)
    o, conv, s, *casts = pl.pallas_call(
        kernel,
        grid=(nb, nchunk),
        input_output_aliases=aliases,
        in_specs=[
            pl.BlockSpec((1, c, GDN_CONV_DIM), lambda b, s: (rows(b, s), 0, 0)),
            pl.BlockSpec((1, c, GDN_V_DIM), lambda b, s: (rows(b, s), 0, z_blk)),
            pl.BlockSpec((1, c, hv), lambda b, s: (rows(b, s), 0, 0)),
            pl.BlockSpec((1, c, hv), lambda b, s: (rows(b, s), 0, 0)),
            pl.BlockSpec((1, hv, c), lambda b, s: (rows(b, s), 0, 0)),
            pl.BlockSpec((1, hv, c), lambda b, s: (rows(b, s), 0, 0)),
            pl.BlockSpec((None, 1, GDN_CONV_W - 1, GDN_CONV_DIM), lambda b, s: (in_slot, b, 0, 0)),
            pl.BlockSpec((None, 1, hv, GDN_DK, GDN_DV), lambda b, s: (in_slot, b, 0, 0, 0)),
            pl.BlockSpec((GDN_CONV_W, GDN_CONV_DIM), lambda b, s: (0, 0)),
            pl.BlockSpec((1, hv), lambda b, s: (0, 0)),
            pl.BlockSpec((1, hv), lambda b, s: (0, 0)),
            pl.BlockSpec((hv, 1), lambda b, s: (0, 0)),
            pl.BlockSpec((hv, 1), lambda b, s: (0, 0)),
            pl.BlockSpec((1, GDN_DV), lambda b, s: (0, 0)),
        ] + cast_in + prev_specs,
        out_specs=[
            pl.BlockSpec((1, c, GDN_V_DIM), lambda b, s: (b * nchunk + s, 0, 0)),
            pl.BlockSpec((None, 1, GDN_CONV_W - 1, GDN_CONV_DIM), lambda b, s: (slot, b, 0, 0)),
            pl.BlockSpec((None, 1, hv, GDN_DK, GDN_DV), lambda b, s: (slot, b, 0, 0, 0)),
        ] + cast_out,
        out_shape=[
            jax.ShapeDtypeStruct((nb * nchunk, c, GDN_V_DIM), BF16),
            jax.ShapeDtypeStruct((n_slots, nb, GDN_CONV_W - 1, GDN_CONV_DIM), F32),
            jax.ShapeDtypeStruct((n_slots, nb, hv, GDN_DK, GDN_DV), F32),
        ] + cast_shape,
        scratch_shapes=[pltpu.VMEM((CONV_PAD + c, GDN_CONV_DIM), F32),
                        pltpu.VMEM((hv, GDN_DK, GDN_DV), F32),
                        pltpu.VMEM((GDN_K_HEADS, 2 * c, c), F32),
                        pltpu.VMEM((hv, c, c), F32),
                        pltpu.VMEM((hv, c, c), F32),
                        pltpu.VMEM((hv, c, c), F32),
                        pltpu.VMEM((hv, c, c), F32),
                        pltpu.VMEM((hv, c, GDN_DV + GDN_DK), F32),
                        pltpu.VMEM((hv, 2 * c, GDN_DK), F32),
                        pltpu.VMEM((hv, c, GDN_DK), F32),
                        pltpu.VMEM((hv, c, GDN_DV), F32)],
        compiler_params=_cparams("parallel", "arbitrary"),
        name=f"gdn_chunk{c}",
    )(proj3, proj3, bpre, apre, bpre_t, apre_t, conv0, s0, conv_w,
      a_log.reshape(1, hv), dt_bias.reshape(1, hv), a_log.reshape(hv, 1), dt_bias.reshape(hv, 1),
      norm_w.reshape(1, GDN_DV), *cast_args, *prev_args)
    return o.reshape(nb * t, GDN_V_DIM), (conv, s), casts


def _mlstm_kernel(main_ref, ig_ref, fg_ref, ig_t_ref, fg_t_ref, c0_ref, n0_ref, m0_ref,
                  bi_ref, bf_ref, bi_t_ref, bf_t_ref, normw_ref, *rest, c, cast):
    c_ref, n_ref, m_ref, qk_ref, qc_ref = rest[-5:]
    if cast:
        _cast_table_slab(rest[0], rest[1], rest[-7], rest[-6])
        o_ref, co_ref, no_ref, mo_ref = rest[-11:-7]
    else:
        o_ref, co_ref, no_ref, mo_ref = rest[-9:-5]
    step = pl.program_id(1)

    @pl.when(step == 0)
    def _():
        c_ref[...] = c0_ref[0]
        n_ref[...] = n0_ref[0]
        m_ref[...] = jnp.broadcast_to(m0_ref[0], m_ref.shape)

    ig_c = ig_ref[0] + bi_ref[...]
    lf_c = _log_sigmoid(fg_ref[0] + bf_ref[...])
    ig_r = ig_t_ref[0] + bi_t_ref[...]
    lf_r = _log_sigmoid(fg_t_ref[0] + bf_t_ref[...])
    causal = _lower(c, strict=False)
    bc_c = _dot_f32(causal.astype(F32), lf_c)
    bc_r = _dot_f32(lf_r, _upper(c).astype(F32))

    for h in range(ML_HEADS):
        q = main_ref[0, :, h * ML_DQK:(h + 1) * ML_DQK]
        k = main_ref[0, :, ML_QK_DIM + h * ML_DQK:ML_QK_DIM + (h + 1) * ML_DQK] * (ML_DQK ** -0.5)
        qk_ref[h] = _dot_nt(q, k)
        qc_ref[h] = _dot(q, c_ref[h])

    for h in range(ML_HEADS):
        q = main_ref[0, :, h * ML_DQK:(h + 1) * ML_DQK]
        k = main_ref[0, :, ML_QK_DIM + h * ML_DQK:ML_QK_DIM + (h + 1) * ML_DQK] * (ML_DQK ** -0.5)
        v = main_ref[0, :, 2 * ML_QK_DIM + h * ML_DV:2 * ML_QK_DIM + (h + 1) * ML_DV]
        o_pre = main_ref[0, :, 2 * ML_QK_DIM + ML_V_DIM + h * ML_DV:2 * ML_QK_DIM + ML_V_DIM + (h + 1) * ML_DV]
        bcol = bc_c[:, h:h + 1]
        brow = bc_r[h:h + 1, :]
        icol = ig_c[:, h:h + 1]
        irow = ig_r[h:h + 1, :]
        m_prev = m_ref[h:h + 1, 0:1]
        cm = c_ref[h]
        nv = n_ref[h:h + 1, :]

        d = jnp.where(causal, bcol - brow + irow, NEG_BIG)
        inter = bcol + m_prev
        m_t = jnp.maximum(inter, jnp.max(d, axis=-1, keepdims=True))
        w_intra = jnp.exp(d - m_t)
        w_inter = jnp.exp(inter - m_t)
        sqk = qk_ref[h] * w_intra
        num = w_inter * qc_ref[h] + _dot(sqk, v)
        den = w_inter * jnp.sum(q * nv, axis=-1, keepdims=True) + jnp.sum(sqk, axis=-1, keepdims=True)
        hid = num / jnp.maximum(jnp.abs(den), jnp.exp(-m_t))

        b_last = bcol[c - 1:c, :]
        d_end = b_last - bcol + icol
        m_new = jnp.maximum(b_last + m_prev, jnp.max(d_end, axis=0, keepdims=True))
        wk = jnp.exp(d_end - m_new) * k
        scale = jnp.exp(b_last + m_prev - m_new)
        c_ref[h] = scale * cm + _dot_tn(wk, v)
        n_ref[h:h + 1, :] = scale * nv + jnp.sum(wk, axis=0, keepdims=True)
        m_ref[h:h + 1, :] = jnp.broadcast_to(m_new, (1, m_ref.shape[1]))

        hid = hid * lax.rsqrt(jnp.mean(hid * hid, axis=-1, keepdims=True) + RMS_EPS) * normw_ref[...]
        o_ref[0, :, h * ML_DV:(h + 1) * ML_DV] = (hid * _sigmoid(o_pre)).astype(o_ref.dtype)

    @pl.when(step == pl.num_programs(1) - 1)
    def _():
        co_ref[0] = c_ref[...]
        no_ref[0] = n_ref[...]
        mo_ref[0] = m_ref[:, 0:1]


def mlstm_group(proj, gates, gates_t, c0, n0, m0, in_slot, gate_b, norm_w, *, row0, nb, t, c,
                slot, n_slots, prev, tables=None, table_layer=0):
    tokens = proj.shape[0]
    nchunk = t // c
    blk0 = row0 // c
    nh = ML_HEADS
    proj3 = proj.reshape(tokens // c, c, proj.shape[1])
    ig = gates[:, :nh].reshape(tokens // c, c, nh)
    fg = gates[:, nh:].reshape(tokens // c, c, nh)
    ig_t = gates_t[:nh].reshape(nh, tokens // c, c).transpose(1, 0, 2)
    fg_t = gates_t[nh:].reshape(nh, tokens // c, c).transpose(1, 0, 2)

    def rows(b, s):
        return blk0 + b * nchunk + s

    kernel = functools.partial(_mlstm_kernel, c=c, cast=tables is not None)
    cast_args, cast_in, cast_out, cast_shape = _cast_specs(tables, table_layer, nb * nchunk,
                                                           lambda b, s: b * nchunk + s)
    prev_args, prev_specs, aliases = _alias_args(prev, first_input=13 + len(cast_args), first_output=1)
    o, cm, nv, m, *casts = pl.pallas_call(
        kernel,
        grid=(nb, nchunk),
        input_output_aliases=aliases,
        in_specs=[
            pl.BlockSpec((1, c, ML_MAIN_DIM), lambda b, s: (rows(b, s), 0, 0)),
            pl.BlockSpec((1, c, nh), lambda b, s: (rows(b, s), 0, 0)),
            pl.BlockSpec((1, c, nh), lambda b, s: (rows(b, s), 0, 0)),
            pl.BlockSpec((1, nh, c), lambda b, s: (rows(b, s), 0, 0)),
            pl.BlockSpec((1, nh, c), lambda b, s: (rows(b, s), 0, 0)),
            pl.BlockSpec((None, 1, nh, ML_DQK, ML_DV), lambda b, s: (in_slot, b, 0, 0, 0)),
            pl.BlockSpec((None, 1, nh, ML_DQK), lambda b, s: (in_slot, b, 0, 0)),
            pl.BlockSpec((None, 1, nh, 1), lambda b, s: (in_slot, b, 0, 0)),
            pl.BlockSpec((1, nh), lambda b, s: (0, 0)),
            pl.BlockSpec((1, nh), lambda b, s: (0, 0)),
            pl.BlockSpec((nh, 1), lambda b, s: (0, 0)),
            pl.BlockSpec((nh, 1), lambda b, s: (0, 0)),
            pl.BlockSpec((1, ML_DV), lambda b, s: (0, 0)),
        ] + cast_in + prev_specs,
        out_specs=[
            pl.BlockSpec((1, c, ML_V_DIM), lambda b, s: (b * nchunk + s, 0, 0)),
            pl.BlockSpec((None, 1, nh, ML_DQK, ML_DV), lambda b, s: (slot, b, 0, 0, 0)),
            pl.BlockSpec((None, 1, nh, ML_DQK), lambda b, s: (slot, b, 0, 0)),
            pl.BlockSpec((None, 1, nh, 1), lambda b, s: (slot, b, 0, 0)),
        ] + cast_out,
        out_shape=[
            jax.ShapeDtypeStruct((nb * nchunk, c, ML_V_DIM), BF16),
            jax.ShapeDtypeStruct((n_slots, nb, nh, ML_DQK, ML_DV), F32),
            jax.ShapeDtypeStruct((n_slots, nb, nh, ML_DQK), F32),
            jax.ShapeDtypeStruct((n_slots, nb, nh, 1), F32),
        ] + cast_shape,
        scratch_shapes=[pltpu.VMEM((nh, ML_DQK, ML_DV), F32),
                        pltpu.VMEM((nh, ML_DQK), F32),
                        pltpu.VMEM((nh, 128), F32),
                        pltpu.VMEM((nh, c, c), F32),
                        pltpu.VMEM((nh, c, ML_DV), F32)],
        compiler_params=_cparams("parallel", "arbitrary"),
        name=f"mlstm_chunk{c}",
    )(proj3, ig, fg, ig_t, fg_t, c0, n0, m0.reshape(m0.shape + (1,)),
      gate_b[:nh].reshape(1, nh), gate_b[nh:].reshape(1, nh),
      gate_b[:nh].reshape(nh, 1), gate_b[nh:].reshape(nh, 1), norm_w.reshape(1, ML_DV), *cast_args, *prev_args)
    return o.reshape(nb * t, ML_V_DIM), (cm, nv, m), casts


_CAND_PAIRS = [(a, b) for a in range(PEER_TOPK) for b in range(PEER_TOPK) if (a + 1) * (b + 1) <= PEER_TOPK]
_CAND_ROWS = -(-len(_CAND_PAIRS) // 8) * 8


def _top_values(work, count):
    rows = lax.broadcasted_iota(jnp.int32, work.shape, 0)
    out = []
    for r in range(count):
        m = jnp.max(work, axis=0, keepdims=True)
        out.append(m)
        if r + 1 < count:
            first = jnp.min(jnp.where(work == m, rows, work.shape[0]), axis=0, keepdims=True)
            work = jnp.where(rows == first, NEG_INF, work)
    return out


def _sorting_network(n):
    pairs = []
    p = 1
    while p < n:
        k = p
        while k >= 1:
            for j in range(k % p, n - k, 2 * k):
                for i in range(min(k, n - j - k)):
                    if (i + j) // (2 * p) == (i + j + k) // (2 * p):
                        pairs.append((i + j, i + j + k))
            k //= 2
        p *= 2
    return pairs


def _compare_exchange(vals, i, j):
    vals[i], vals[j] = jnp.maximum(vals[i], vals[j]), jnp.minimum(vals[i], vals[j])


def _top16_of_128(s):
    k = PEER_TOPK
    slab = s.shape[0] // k
    vals = [s[i * slab:(i + 1) * slab, :] for i in range(k)]
    for i, j in _sorting_network(k):
        _compare_exchange(vals, i, j)
    shift = slab // 2
    while shift >= 1:
        vals = [jnp.maximum(vals[i], pltpu.roll(vals[k - 1 - i], shift, 0)) for i in range(k)]
        stride = k // 2
        while stride >= 1:
            for i in range(k):
                if (i // stride) % 2 == 0:
                    _compare_exchange(vals, i, i + stride)
            stride //= 2
        shift //= 2
    return [v[0:1, :] for v in vals]


def _peer_route_kernel(q_ref, keys_ref, th1_ref, w1_ref, s2_ref, w2_ref, cand_ref):
    cand_ref[...] = jnp.full(cand_ref.shape, NEG_INF, F32)
    for h in range(PEER_HEADS):
        scores, tops = [], []
        for p in range(2):
            col = (2 * h + p) * PEER_HALF
            s = _dot_nt(keys_ref[h, p], q_ref[:, col:col + PEER_HALF])
            scores.append(s)
            tops.append(_top16_of_128(s))
        for i, (a, b) in enumerate(_CAND_PAIRS):
            cand_ref[i:i + 1, :] = tops[0][a] + tops[1][b]
        cand = cand_ref[...]
        tau = _top_values(cand, PEER_TOPK)[-1]
        max1, max2 = tops[0][0], tops[1][0]
        z = jnp.sum(jnp.where(cand >= tau, jnp.exp(cand - (max1 + max2)), 0.0), axis=0, keepdims=True)
        cut = jnp.full(scores[0].shape, POS_INF, F32)
        for a in range(PEER_TOPK):
            cut_a = jnp.full(tau.shape, POS_INF, F32)
            for b in range(PEER_TOPK // (a + 1)):
                cut_a = jnp.where(tops[0][a] + tops[1][b] >= tau, tops[1][b], cut_a)
            cut = jnp.where(scores[0] == tops[0][a], cut_a, cut)
        th1_ref[h] = cut
        w1_ref[h] = jnp.exp(scores[0] - max1)
        s2_ref[h] = scores[1]
        w2_ref[h] = jnp.exp(scores[1] - max2) / z


def peer_route(q, keys, *, tm):
    m = q.shape[0]
    nh = PEER_HEADS
    table = jax.ShapeDtypeStruct((nh, PEER_N_KEYS, m), F32)
    table_spec = pl.BlockSpec((nh, PEER_N_KEYS, tm), lambda i: (0, 0, i))
    return pl.pallas_call(
        _peer_route_kernel,
        grid=(m // tm,),
        in_specs=[pl.BlockSpec((tm, q.shape[1]), lambda i: (i, 0)),
                  pl.BlockSpec(keys.shape, lambda i: (0, 0, 0, 0))],
        out_specs=[table_spec] * 4,
        out_shape=[table] * 4,
        scratch_shapes=[pltpu.VMEM((_CAND_ROWS, tm), F32)],
        compiler_params=_cparams("parallel"),
        name="peer_route",
    )(q, keys)


def _peer_main_kernel(xb_ref, u_ref, vt_ref, th1_ref, w1_ref, s2_ref, w2_ref, x_ref, g_ref, b_ref,
                      o_ref, ob_ref, acc_ref, ht_ref, a_ref, *, te):
    e = pl.program_id(1)
    groups = te // PEER_N_KEYS
    tm = ht_ref.shape[1]

    @pl.when(e == 0)
    def _():
        acc_ref[...] = jnp.zeros_like(acc_ref)

    ht_ref[...] = _dot_nt(u_ref[...], xb_ref[...])
    for cc in range(groups):
        key1 = e * groups + cc
        rows = slice(cc * PEER_N_KEYS, (cc + 1) * PEER_N_KEYS)
        for t0 in range(0, tm, 2 * LANE_TILE):
            pair = slice(t0, t0 + 2 * LANE_TILE)
            th_rows = [th1_ref[h, pl.ds(key1, 1), pair] for h in range(PEER_HEADS)]
            w1_rows = [w1_ref[h, pl.ds(key1, 1), pair] for h in range(PEER_HEADS)]
            for half in range(2):
                lanes = slice(t0 + half * LANE_TILE, t0 + (half + 1) * LANE_TILE)
                sub = slice(half * LANE_TILE, (half + 1) * LANE_TILE)
                gate = jnp.zeros((PEER_N_KEYS, LANE_TILE), F32)
                for h in range(PEER_HEADS):
                    hit = s2_ref[h, :, lanes] >= th_rows[h][:, sub]
                    gate = gate + jnp.where(hit, w2_ref[h, :, lanes], 0.0) * w1_rows[h][:, sub]
                a_ref[rows, lanes] = (_gelu_tanh(ht_ref[rows, lanes]) * gate).astype(BF16)
    acc_ref[...] += _dot(vt_ref[...], a_ref[...])

    @pl.when(e == pl.num_programs(1) - 1)
    def _():
        for t0 in range(0, tm, LANE_TILE):
            tok = slice(t0, t0 + LANE_TILE)
            y = _layer_norm_rows(DEEPNORM_ALPHA * x_ref[tok, :] + acc_ref[:, tok].T, g_ref[...], b_ref[...])
            o_ref[tok, :] = y
            ob_ref[tok, :] = y.astype(BF16)


def peer_main(xb, x, u, vt, th1, w1, s2, w2, g, b, *, tm, te):
    m, d = x.shape
    ne = u.shape[0]
    nh = PEER_HEADS
    kernel = functools.partial(_peer_main_kernel, te=te)
    once = pl.Buffered(1)
    table_spec = pl.BlockSpec((nh, PEER_N_KEYS, tm), lambda i, e: (0, 0, i), pipeline_mode=once)
    return pl.pallas_call(
        kernel,
        grid=(m // tm, ne // te),
        in_specs=[pl.BlockSpec((tm, d), lambda i, e: (i, 0), pipeline_mode=once),
                  pl.BlockSpec((te, d), lambda i, e: (e, 0)),
                  pl.BlockSpec((d, te), lambda i, e: (0, e)),
                  table_spec, table_spec, table_spec, table_spec,
                  pl.BlockSpec((tm, d), lambda i, e: (i, 0), pipeline_mode=once),
                  pl.BlockSpec((1, d), lambda i, e: (0, 0)),
                  pl.BlockSpec((1, d), lambda i, e: (0, 0))],
        out_specs=[pl.BlockSpec((tm, d), lambda i, e: (i, 0), pipeline_mode=once),
                   pl.BlockSpec((tm, d), lambda i, e: (i, 0), pipeline_mode=once)],
        out_shape=[jax.ShapeDtypeStruct((m, d), F32), jax.ShapeDtypeStruct((m, d), BF16)],
        scratch_shapes=[pltpu.VMEM((d, tm), F32), pltpu.VMEM((te, tm), F32), pltpu.VMEM((te, tm), BF16)],
        compiler_params=_cparams("parallel", "arbitrary"),
        name="peer_main",
    )(xb, u, vt, th1, w1, s2, w2, x, g.reshape(1, d), b.reshape(1, d))


TOKEN_TILE = 3072
GATE_TOKEN_TILE = 1024
PROJ_COL_TILE = 512
OUT_K_TILE = 512
OUT_ROW_TILE = 1024
PEER_TOKEN_TILE = 512
PEER_EXPERT_TILE = 1024


def kernel(x_prompt, x_sample, state_gdn_conv, state_gdn_s, state_mlstm_c, state_mlstm_n, state_mlstm_m,
           gdn_w_in, gdn_conv_w, gdn_a_log, gdn_dt_bias, gdn_norm_w, gdn_w_out,
           ml_w_in, ml_gate_b, ml_norm_w, ml_w_out,
           ln_mix_g, ln_mix_b, ln_ffn_g, ln_ffn_b,
           peer_w_q, peer_keys, peer_u, peer_v):
    bp, tp, d = x_prompt.shape
    bs, ts, _ = x_sample.shape
    np_tok = bp * tp
    ns_tok = bs * ts
    x = jnp.concatenate([x_prompt.reshape(np_tok, d), x_sample.reshape(ns_tok, d)], axis=0)
    xb = x.astype(BF16)
    cp = min(GDN_CHUNK, tp)
    cs = min(GDN_CHUNK, ts)

    n_gdn = state_gdn_s.shape[0]
    n_ml = state_mlstm_c.shape[0]
    zero_gdn = (jnp.zeros((1, bp) + state_gdn_conv.shape[2:], F32), jnp.zeros((1, bp) + state_gdn_s.shape[2:], F32))
    zero_ml = (jnp.zeros((1, bp) + state_mlstm_c.shape[2:], F32), jnp.zeros((1, bp) + state_mlstm_n.shape[2:], F32),
               jnp.zeros((1, bp) + state_mlstm_m.shape[2:], F32))
    p_gdn = s_gdn = p_ml = s_ml = None
    for layer in range(DEPTH):
        j = layer // 2
        prompt = dict(row0=0, nb=bp, t=tp, c=cp, slot=j)
        sample = dict(row0=np_tok, nb=bs, t=ts, c=cs, slot=j)
        if layer % 2 == 0:
            proj = matmul(xb, gdn_w_in, j, GDN_MAIN_DIM, tm=TOKEN_TILE, tn=PROJ_COL_TILE)
            gates, gates_t = gate_proj(xb, gdn_w_in[j, :, GDN_MAIN_DIM:], tm=GATE_TOKEN_TILE)
            shared = (gdn_conv_w[j], gdn_a_log[j], gdn_dt_bias[j], gdn_norm_w[j])
            o_p, p_gdn, (u_bf16, vt_bf16) = gdn_group(proj, gates, gates_t, *zero_gdn, 0, *shared, n_slots=n_gdn,
                                                      prev=p_gdn, tables=(peer_u, peer_v), table_layer=layer, **prompt)
            o_s, s_gdn, _ = gdn_group(proj, gates, gates_t, state_gdn_conv, state_gdn_s, j, *shared,
                                      n_slots=n_gdn, prev=s_gdn, **sample)
            w_out = gdn_w_out
        else:
            proj = matmul(xb, ml_w_in, j, ML_MAIN_DIM, tm=TOKEN_TILE, tn=PROJ_COL_TILE)
            gates, gates_t = gate_proj(xb, ml_w_in[j, :, ML_MAIN_DIM:], tm=GATE_TOKEN_TILE)
            shared = (ml_gate_b[j], ml_norm_w[j])
            o_p, p_ml, (u_bf16, vt_bf16) = mlstm_group(proj, gates, gates_t, *zero_ml, 0, *shared, n_slots=n_ml,
                                                       prev=p_ml, tables=(peer_u, peer_v), table_layer=layer, **prompt)
            o_s, s_ml, _ = mlstm_group(proj, gates, gates_t, state_mlstm_c, state_mlstm_n, state_mlstm_m, j, *shared,
                                       n_slots=n_ml, prev=s_ml, **sample)
            w_out = ml_w_out
        x, xb = out_proj_ln(o_p, o_s, w_out, j, x, ln_mix_g[layer], ln_mix_b[layer],
                            tm=OUT_ROW_TILE, tk=OUT_K_TILE)

        q = matmul(xb, peer_w_q, layer, peer_w_q.shape[2], tm=TOKEN_TILE, tn=PROJ_COL_TILE)
        tables = peer_route(q, peer_keys[layer], tm=PEER_TOKEN_TILE)
        x, xb = peer_main(xb, x, u_bf16, vt_bf16, *tables,
                          ln_ffn_g[layer], ln_ffn_b[layer], tm=PEER_TOKEN_TILE, te=PEER_EXPERT_TILE)

    y_prompt = x[:np_tok].reshape(bp, tp, d)
    y_sample = x[np_tok:].reshape(bs, ts, d)
    return (y_prompt, y_sample,
            p_gdn[0], p_gdn[1], p_ml[0], p_ml[1], p_ml[2].reshape(n_ml, bp, ML_HEADS),
            s_gdn[0], s_gdn[1], s_ml[0], s_ml[1], s_ml[2].reshape(n_ml, bs, ML_HEADS))
```

```python
import functools

import jax
import jax.numpy as jnp
from jax import lax
from jax.experimental import pallas as pl
from jax.experimental.pallas import tpu as pltpu

F32 = jnp.float32
BF16 = jnp.bfloat16

D_MODEL = 2048
DEPTH = 4
GDN_K_HEADS = 16
GDN_V_HEADS = 32
GDN_DK = 128
GDN_DV = 128
GDN_QK_DIM = GDN_K_HEADS * GDN_DK
GDN_V_DIM = GDN_V_HEADS * GDN_DV
GDN_CONV_DIM = 2 * GDN_QK_DIM + GDN_V_DIM
GDN_MAIN_DIM = GDN_CONV_DIM + GDN_V_DIM
GDN_CONV_W = 4
GDN_CHUNK = 64
ML_HEADS = 8
ML_DQK = 128
ML_DV = 256
ML_QK_DIM = ML_HEADS * ML_DQK
ML_V_DIM = ML_HEADS * ML_DV
ML_MAIN_DIM = 2 * ML_QK_DIM + 2 * ML_V_DIM
ML_CHUNK = 64
PEER_HEADS = 8
PEER_N_KEYS = 128
PEER_HALF = 128
PEER_TOPK = 16
DEEPNORM_ALPHA = (2 * DEPTH) ** 0.25
LN_EPS = 1e-5
RMS_EPS = 1e-6
NEG_BIG = -1e30
NEG_INF = float("-inf")
POS_INF = float("inf")

VMEM_LIMIT_BYTES = 56 * 1024 * 1024
LANE_TILE = 128
HIGHEST = lax.Precision.HIGHEST


def _cparams(*sem):
    return pltpu.CompilerParams(dimension_semantics=sem, vmem_limit_bytes=VMEM_LIMIT_BYTES)


def _dot(a, b):
    return jnp.dot(a.astype(BF16), b.astype(BF16), preferred_element_type=F32)


def _dot_nt(a, b):
    return lax.dot_general(a.astype(BF16), b.astype(BF16), (((1,), (1,)), ((), ())),
                           preferred_element_type=F32)


def _dot_tn(a, b):
    return lax.dot_general(a.astype(BF16), b.astype(BF16), (((0,), (0,)), ((), ())),
                           preferred_element_type=F32)


def _dot_f32(a, b):
    return jnp.dot(a, b, preferred_element_type=F32, precision=HIGHEST)


def _sigmoid(x):
    return 1.0 / (1.0 + jnp.exp(-x))


def _silu(x):
    return x * _sigmoid(x)


def _softplus(x):
    return jnp.maximum(x, 0.0) + jnp.log(1.0 + jnp.exp(-jnp.abs(x)))


def _log_sigmoid(x):
    return -_softplus(-x)


def _gelu_tanh(x):
    return 0.5 * x * (1.0 + jnp.tanh(0.7978845608028654 * (x + 0.044715 * (x * x * x))))


def _lower(c, strict):
    r = lax.broadcasted_iota(jnp.int32, (c, c), 0)
    k = lax.broadcasted_iota(jnp.int32, (c, c), 1)
    return (r > k) if strict else (r >= k)


def _upper(c):
    r = lax.broadcasted_iota(jnp.int32, (c, c), 0)
    k = lax.broadcasted_iota(jnp.int32, (c, c), 1)
    return r <= k


def _layer_norm_rows(v, g, b):
    mu = jnp.mean(v, axis=-1, keepdims=True)
    d = v - mu
    var = jnp.mean(d * d, axis=-1, keepdims=True)
    return d * lax.rsqrt(var + LN_EPS) * g + b


def _matmul_kernel(x_ref, w_ref, o_ref):
    o_ref[...] = _dot(x_ref[...], w_ref[...])


def matmul(x, w_stack, layer, n_cols, *, tm, tn):
    m, k = x.shape
    return pl.pallas_call(
        _matmul_kernel,
        grid=(m // tm, n_cols // tn),
        in_specs=[pl.BlockSpec((tm, k), lambda i, j: (i, 0), pipeline_mode=pl.Buffered(1)),
                  pl.BlockSpec((None, k, tn), lambda i, j: (layer, 0, j))],
        out_specs=pl.BlockSpec((tm, tn), lambda i, j: (i, j)),
        out_shape=jax.ShapeDtypeStruct((m, n_cols), F32),
        compiler_params=_cparams("parallel", "arbitrary"),
        name="proj_matmul",
    )(x, w_stack)


def _gate_proj_kernel(x_ref, w_ref, wt_ref, o_ref, ot_ref):
    x = x_ref[...]
    o_ref[...] = _dot(x, w_ref[...])
    ot_ref[...] = _dot_nt(wt_ref[...], x)


def gate_proj(x, w_gate, *, tm):
    m, k = x.shape
    n = w_gate.shape[1]
    return pl.pallas_call(
        _gate_proj_kernel,
        grid=(m // tm,),
        in_specs=[pl.BlockSpec((tm, k), lambda i: (i, 0)),
                  pl.BlockSpec((k, n), lambda i: (0, 0)),
                  pl.BlockSpec((n, k), lambda i: (0, 0))],
        out_specs=[pl.BlockSpec((tm, n), lambda i: (i, 0)),
                   pl.BlockSpec((n, tm), lambda i: (0, i))],
        out_shape=[jax.ShapeDtypeStruct((m, n), F32), jax.ShapeDtypeStruct((n, m), F32)],
        compiler_params=_cparams("parallel"),
        name="gate_proj",
    )(x, w_gate, w_gate.T)


def _out_proj_ln_kernel(hp_ref, hs_ref, w_ref, x_ref, g_ref, b_ref, o_ref, ob_ref, y_ref, *, prompt_tiles):
    i = pl.program_id(0)
    n = pl.program_id(1)

    @pl.when(i < prompt_tiles)
    def _():
        y_ref[n] = _dot(hp_ref[...], w_ref[...])

    @pl.when(i >= prompt_tiles)
    def _():
        y_ref[n] = _dot(hs_ref[...], w_ref[...])

    @pl.when(n == pl.num_programs(1) - 1)
    def _():
        mix = jnp.concatenate([y_ref[j] for j in range(y_ref.shape[0])], axis=-1)
        y = _layer_norm_rows(DEEPNORM_ALPHA * x_ref[...] + mix, g_ref[...], b_ref[...])
        o_ref[...] = y
        ob_ref[...] = y.astype(BF16)


def out_proj_ln(h_prompt, h_sample, w_stack, layer, x, g, b, *, tm, tn):
    mp, k = h_prompt.shape
    ms = h_sample.shape[0]
    d = w_stack.shape[2]
    prompt_tiles = mp // tm
    once = pl.Buffered(1)
    kernel = functools.partial(_out_proj_ln_kernel, prompt_tiles=prompt_tiles)
    return pl.pallas_call(
        kernel,
        grid=((mp + ms) // tm, d // tn),
        in_specs=[pl.BlockSpec((tm, k), lambda i, j: (jnp.minimum(i, prompt_tiles - 1), 0), pipeline_mode=once),
                  pl.BlockSpec((tm, k), lambda i, j: (jnp.maximum(i - prompt_tiles, 0), 0), pipeline_mode=once),
                  pl.BlockSpec((None, k, tn), lambda i, j: (layer, 0, j)),
                  pl.BlockSpec((tm, d), lambda i, j: (i, 0), pipeline_mode=once),
                  pl.BlockSpec((1, d), lambda i, j: (0, 0)),
                  pl.BlockSpec((1, d), lambda i, j: (0, 0))],
        out_specs=[pl.BlockSpec((tm, d), lambda i, j: (i, 0), pipeline_mode=once),
                   pl.BlockSpec((tm, d), lambda i, j: (i, 0), pipeline_mode=once)],
        out_shape=[jax.ShapeDtypeStruct((mp + ms, d), F32), jax.ShapeDtypeStruct((mp + ms, d), BF16)],
        scratch_shapes=[pltpu.VMEM((d // tn, tm, tn), F32)],
        compiler_params=_cparams("parallel", "arbitrary"),
        name="out_proj_ln",
    )(h_prompt, h_sample, w_stack, x, g.reshape(1, d), b.reshape(1, d))


CONV_PAD = 8


def _dot_split(e, r):
    c = e.shape[0]
    e_hi = e.astype(BF16).astype(F32)
    r_hi = r.astype(BF16).astype(F32)
    stacked = _dot(jnp.concatenate([e_hi, e - e_hi], axis=0), r_hi)
    return stacked[0:c] + stacked[c:2 * c] + _dot(e_hi, r - r_hi)


def _cast_table_slab(u_ref, v_ref, ub_ref, vt_ref):
    ub_ref[...] = u_ref[...].astype(BF16)
    vt_ref[...] = v_ref[...].T.astype(BF16)


def _cast_specs(tables, layer, steps, step_index):
    if tables is None:
        return [], [], [], []
    ne, d = tables[0].shape[1:]
    slab = ne // steps
    in_specs = [pl.BlockSpec((None, slab, d), lambda b, s: (layer, step_index(b, s), 0))] * 2
    out_specs = [pl.BlockSpec((slab, d), lambda b, s: (step_index(b, s), 0)),
                 pl.BlockSpec((d, slab), lambda b, s: (0, step_index(b, s)))]
    out_shape = [jax.ShapeDtypeStruct((ne, d), BF16), jax.ShapeDtypeStruct((d, ne), BF16)]
    return list(tables), in_specs, out_specs, out_shape


def _gdn_kernel(qkv_ref, z_ref, bpre_ref, apre_ref, bpre_t_ref, apre_t_ref, conv0_ref, s0_ref,
                convw_ref, alog_ref, dtb_ref, alog_t_ref, dtb_t_ref, normw_ref, *rest, c, cast):
    (xp_ref, s_ref, kq_ref, l_ref, a_ref, e_ref, p_ref, rhs_ref, lhs_ref, kd_ref, vn_ref) = rest[-11:]
    if cast:
        _cast_table_slab(rest[0], rest[1], rest[-13], rest[-12])
        o_ref, convo_ref, so_ref = rest[-16:-13]
    else:
        o_ref, convo_ref, so_ref = rest[-14:-11]
    step = pl.program_id(1)
    hist = GDN_CONV_W - 1
    heads = range(GDN_V_HEADS)
    rep = GDN_V_HEADS // GDN_K_HEADS
    tok = slice(CONV_PAD, CONV_PAD + c)

    @pl.when(step == 0)
    def _():
        xp_ref[CONV_PAD - hist:CONV_PAD, :] = conv0_ref[0]
        s_ref[...] = s0_ref[0]

    x = qkv_ref[0]
    xp_ref[tok, :] = x
    y = xp_ref[CONV_PAD - hist:CONV_PAD - hist + c, :] * convw_ref[0:1, :]
    for i in range(1, GDN_CONV_W):
        y = y + xp_ref[CONV_PAD - hist + i:CONV_PAD - hist + i + c, :] * convw_ref[i:i + 1, :]
    tail = x[c - hist:c, :]
    xp_ref[CONV_PAD - hist:CONV_PAD, :] = tail
    convo_ref[0] = tail
    xp_ref[tok, :] = _silu(y)

    beta_c = _sigmoid(bpre_ref[0])
    g_c = -jnp.exp(alog_ref[...]) * _softplus(apre_ref[0] + dtb_ref[...])
    g_r = -jnp.exp(alog_t_ref[...]) * _softplus(apre_t_ref[0] + dtb_t_ref[...])
    rows = lax.broadcasted_iota(jnp.int32, (c, c), 0)
    cols = lax.broadcasted_iota(jnp.int32, (c, c), 1)
    causal = rows >= cols
    strict = rows > cols
    gc_c = _dot_f32(causal.astype(F32), g_c)
    gc_r = _dot_f32(g_r, (rows <= cols).astype(F32))

    for kh in range(GDN_K_HEADS):
        qs = slice(kh * GDN_DK, (kh + 1) * GDN_DK)
        ks = slice(GDN_QK_DIM + kh * GDN_DK, GDN_QK_DIM + (kh + 1) * GDN_DK)
        q = xp_ref[tok, qs]
        k = xp_ref[tok, ks]
        q = q * lax.rsqrt(jnp.sum(q * q, axis=-1, keepdims=True) + RMS_EPS) * (GDN_DK ** -0.5)
        k = k * lax.rsqrt(jnp.sum(k * k, axis=-1, keepdims=True) + RMS_EPS)
        xp_ref[tok, qs] = q
        xp_ref[tok, ks] = k
        kq_ref[kh] = _dot_nt(jnp.concatenate([k, q], axis=0), k)

    diag8 = strict & ((rows >> 3) == (cols >> 3))
    for h in heads:
        kh = h // rep
        q = xp_ref[tok, kh * GDN_DK:(kh + 1) * GDN_DK]
        k = xp_ref[tok, GDN_QK_DIM + kh * GDN_DK:GDN_QK_DIM + (kh + 1) * GDN_DK]
        v = xp_ref[tok, 2 * GDN_QK_DIM + h * GDN_DV:2 * GDN_QK_DIM + (h + 1) * GDN_DV]
        gcol = gc_c[:, h:h + 1]
        bcol = beta_c[:, h:h + 1]
        diff = gcol - gc_r[h:h + 1, :]
        egc = jnp.exp(gcol)
        l = bcol * kq_ref[kh, 0:c, :] * jnp.exp(jnp.where(strict, diff, NEG_BIG))
        l_ref[h] = l
        a_ref[h] = kq_ref[kh, c:2 * c, :] * jnp.exp(jnp.where(causal, diff, NEG_BIG))
        rhs_ref[h, :, 0:GDN_DV] = v * bcol
        rhs_ref[h, :, GDN_DV:GDN_DV + GDN_DK] = k * (bcol * egc)
        lhs_ref[h, c:2 * c, :] = q * egc
        kd_ref[h] = k * jnp.exp(gcol[c - 1:c, :] - gcol)
        l8 = jnp.where(diag8, l, 0.0)
        e_ref[h] = -l8
        p_ref[h] = _dot(l8, l8)
    for h in heads:
        e = e_ref[h]
        p = p_ref[h]
        e_ref[h] = e + p + _dot(e, p)
        p_ref[h] = _dot(p, p)
    for h in heads:
        e = e_ref[h]
        p = p_ref[h]
        e_ref[h] = e + p + _dot(e, p)
    shift = 3
    while (2 << shift) <= c:
        lower_left = (((rows >> (shift + 1)) == (cols >> (shift + 1)))
                      & (((rows >> shift) & 1) == 1) & (((cols >> shift) & 1) == 0))
        for h in heads:
            n = jnp.where(lower_left, l_ref[h], 0.0)
            p_ref[h] = n + _dot(e_ref[h], n)
        for h in heads:
            t = p_ref[h]
            e = e_ref[h]
            e_ref[h] = e - t - _dot(t, e)
        shift += 1

    for h in heads:
        r = rhs_ref[h]
        uw = r + _dot_split(e_ref[h], r)
        rhs_ref[h, :, 0:GDN_DV] = uw[:, 0:GDN_DV]
        lhs_ref[h, 0:c, :] = uw[:, GDN_DV:GDN_DV + GDN_DK]
    for h in heads:
        ws = _dot(lhs_ref[h], s_ref[h])
        vn_ref[h] = rhs_ref[h, :, 0:GDN_DV] - ws[0:c]
        lhs_ref[h, c:2 * c, :] = ws[c:2 * c]
    for h in heads:
        v_new = vn_ref[h]
        o = lhs_ref[h, c:2 * c, :] + _dot(a_ref[h], v_new)
        s_ref[h] = s_ref[h] * jnp.exp(gc_c[c - 1:c, h:h + 1]) + _dot_tn(kd_ref[h], v_new)
        o = o * lax.rsqrt(jnp.mean(o * o, axis=-1, keepdims=True) + RMS_EPS) * normw_ref[...]
        zh = z_ref[0, :, h * GDN_DV:(h + 1) * GDN_DV]
        o_ref[0, :, h * GDN_DV:(h + 1) * GDN_DV] = (o * _silu(zh)).astype(o_ref.dtype)

    @pl.when(step == pl.num_programs(1) - 1)
    def _():
        so_ref[0] = s_ref[...]


def _alias_args(prev, first_input, first_output):
    if prev is None:
        return [], [], {}
    specs = [pl.BlockSpec(memory_space=pl.ANY)] * len(prev)
    aliases = {first_input + i: first_output + i for i in range(len(prev))}
    return list(prev), specs, aliases


def gdn_group(proj, gates, gates_t, conv0, s0, in_slot, conv_w, a_log, dt_bias, norm_w, *, row0, nb, t, c,
              slot, n_slots, prev, tables=None, table_layer=0):
    tokens = proj.shape[0]
    nchunk = t // c
    blk0 = row0 // c
    hv = GDN_V_HEADS
    proj3 = proj.reshape(tokens // c, c, proj.shape[1])
    bpre = gates[:, :hv].reshape(tokens // c, c, hv)
    apre = gates[:, hv:].reshape(tokens // c, c, hv)
    bpre_t = gates_t[:hv].reshape(hv, tokens // c, c).transpose(1, 0, 2)
    apre_t = gates_t[hv:].reshape(hv, tokens // c, c).transpose(1, 0, 2)
    z_blk = GDN_CONV_DIM // GDN_V_DIM

    def rows(b, s):
        return blk0 + b * nchunk + s

    kernel = functools.partial(_gdn_kernel, c=c, cast=tables is not None)
    cast_args, cast_in, cast_out, cast_shape = _cast_specs(tables, table_layer, nb * nchunk,
                                                           lambda b, s: b * nchunk + s)
    prev_args, prev_specs, aliases = _alias_args(prev, first_input=14 + len(cast_args), first_output=1)
    o, conv, s, *casts = pl.pallas_call(
        kernel,
        grid=(nb, nchunk),
        input_output_aliases=aliases,
        in_specs=[
            pl.BlockSpec((1, c, GDN_CONV_DIM), lambda b, s: (rows(b, s), 0, 0)),
            pl.BlockSpec((1, c, GDN_V_DIM), lambda b, s: (rows(b, s), 0, z_blk)),
            pl.BlockSpec((1, c, hv), lambda b, s: (rows(b, s), 0, 0)),
            pl.BlockSpec((1, c, hv), lambda b, s: (rows(b, s), 0, 0)),
            pl.BlockSpec((1, hv, c), lambda b, s: (rows(b, s), 0, 0)),
            pl.BlockSpec((1, hv, c), lambda b, s: (rows(b, s), 0, 0)),
            pl.BlockSpec((None, 1, GDN_CONV_W - 1, GDN_CONV_DIM), lambda b, s: (in_slot, b, 0, 0)),
            pl.BlockSpec((None, 1, hv, GDN_DK, GDN_DV), lambda b, s: (in_slot, b, 0, 0, 0)),
            pl.BlockSpec((GDN_CONV_W, GDN_CONV_DIM), lambda b, s: (0, 0)),
            pl.BlockSpec((1, hv), lambda b, s: (0, 0)),
            pl.BlockSpec((1, hv), lambda b, s: (0, 0)),
            pl.BlockSpec((hv, 1), lambda b, s: (0, 0)),
            pl.BlockSpec((hv, 1), lambda b, s: (0, 0)),
            pl.BlockSpec((1, GDN_DV), lambda b, s: (0, 0)),
        ] + cast_in + prev_specs,
        out_specs=[
            pl.BlockSpec((1, c, GDN_V_DIM), lambda b, s: (b * nchunk + s, 0, 0)),
            pl.BlockSpec((None, 1, GDN_CONV_W - 1, GDN_CONV_DIM), lambda b, s: (slot, b, 0, 0)),
            pl.BlockSpec((None, 1, hv, GDN_DK, GDN_DV), lambda b, s: (slot, b, 0, 0, 0)),
        ] + cast_out,
        out_shape=[
            jax.ShapeDtypeStruct((nb * nchunk, c, GDN_V_DIM), BF16),
            jax.ShapeDtypeStruct((n_slots, nb, GDN_CONV_W - 1, GDN_CONV_DIM), F32),
            jax.ShapeDtypeStruct((n_slots, nb, hv, GDN_DK, GDN_DV), F32),
        ] + cast_shape,
        scratch_shapes=[pltpu.VMEM((CONV_PAD + c, GDN_CONV_DIM), F32),
                        pltpu.VMEM((hv, GDN_DK, GDN_DV), F32),
                        pltpu.VMEM((GDN_K_HEADS, 2 * c, c), F32),
                        pltpu.VMEM((hv, c, c), F32),
                        pltpu.VMEM((hv, c, c), F32),
                        pltpu.VMEM((hv, c, c), F32),
                        pltpu.VMEM((hv, c, c), F32),
                        pltpu.VMEM((hv, c, GDN_DV + GDN_DK), F32),
                        pltpu.VMEM((hv, 2 * c, GDN_DK), F32),
                        pltpu.VMEM((hv, c, GDN_DK), F32),
                        pltpu.VMEM((hv, c, GDN_DV), F32)],
        compiler_params=_cparams("parallel", "arbitrary"),
        name=f"gdn_chunk{c}",
    )(proj3, proj3, bpre, apre, bpre_t, apre_t, conv0, s0, conv_w,
      a_log.reshape(1, hv), dt_bias.reshape(1, hv), a_log.reshape(hv, 1), dt_bias.reshape(hv, 1),
      norm_w.reshape(1, GDN_DV), *cast_args, *prev_args)
    return o.reshape(nb * t, GDN_V_DIM), (conv, s), casts


def _mlstm_kernel(main_ref, ig_ref, fg_ref, ig_t_ref, fg_t_ref, c0_ref, n0_ref, m0_ref,
                  bi_ref, bf_ref, bi_t_ref, bf_t_ref, normw_ref, *rest, c, cast):
    c_ref, n_ref, m_ref, qk_ref, qc_ref = rest[-5:]
    if cast:
        _cast_table_slab(rest[0], rest[1], rest[-7], rest[-6])
        o_ref, co_ref, no_ref, mo_ref = rest[-11:-7]
    else:
        o_ref, co_ref, no_ref, mo_ref = rest[-9:-5]
    step = pl.program_id(1)

    @pl.when(step == 0)
    def _():
        c_ref[...] = c0_ref[0]
        n_ref[...] = n0_ref[0]
        m_ref[...] = jnp.broadcast_to(m0_ref[0], m_ref.shape)

    ig_c = ig_ref[0] + bi_ref[...]
    lf_c = _log_sigmoid(fg_ref[0] + bf_ref[...])
    ig_r = ig_t_ref[0] + bi_t_ref[...]
    lf_r = _log_sigmoid(fg_t_ref[0] + bf_t_ref[...])
    causal = _lower(c, strict=False)
    bc_c = _dot_f32(causal.astype(F32), lf_c)
    bc_r = _dot_f32(lf_r, _upper(c).astype(F32))

    for h in range(ML_HEADS):
        q = main_ref[0, :, h * ML_DQK:(h + 1) * ML_DQK]
        k = main_ref[0, :, ML_QK_DIM + h * ML_DQK:ML_QK_DIM + (h + 1) * ML_DQK] * (ML_DQK ** -0.5)
        qk_ref[h] = _dot_nt(q, k)
        qc_ref[h] = _dot(q, c_ref[h])

    for h in range(ML_HEADS):
        q = main_ref[0, :, h * ML_DQK:(h + 1) * ML_DQK]
        k = main_ref[0, :, ML_QK_DIM + h * ML_DQK:ML_QK_DIM + (h + 1) * ML_DQK] * (ML_DQK ** -0.5)
        v = main_ref[0, :, 2 * ML_QK_DIM + h * ML_DV:2 * ML_QK_DIM + (h + 1) * ML_DV]
        o_pre = main_ref[0, :, 2 * ML_QK_DIM + ML_V_DIM + h * ML_DV:2 * ML_QK_DIM + ML_V_DIM + (h + 1) * ML_DV]
        bcol = bc_c[:, h:h + 1]
        brow = bc_r[h:h + 1, :]
        icol = ig_c[:, h:h + 1]
        irow = ig_r[h:h + 1, :]
        m_prev = m_ref[h:h + 1, 0:1]
        cm = c_ref[h]
        nv = n_ref[h:h + 1, :]

        d = jnp.where(causal, bcol - brow + irow, NEG_BIG)
        inter = bcol + m_prev
        m_t = jnp.maximum(inter, jnp.max(d, axis=-1, keepdims=True))
        w_intra = jnp.exp(d - m_t)
        w_inter = jnp.exp(inter - m_t)
        sqk = qk_ref[h] * w_intra
        num = w_inter * qc_ref[h] + _dot(sqk, v)
        den = w_inter * jnp.sum(q * nv, axis=-1, keepdims=True) + jnp.sum(sqk, axis=-1, keepdims=True)
        hid = num / jnp.maximum(jnp.abs(den), jnp.exp(-m_t))

        b_last = bcol[c - 1:c, :]
        d_end = b_last - bcol + icol
        m_new = jnp.maximum(b_last + m_prev, jnp.max(d_end, axis=0, keepdims=True))
        wk = jnp.exp(d_end - m_new) * k
        scale = jnp.exp(b_last + m_prev - m_new)
        c_ref[h] = scale * cm + _dot_tn(wk, v)
        n_ref[h:h + 1, :] = scale * nv + jnp.sum(wk, axis=0, keepdims=True)
        m_ref[h:h + 1, :] = jnp.broadcast_to(m_new, (1, m_ref.shape[1]))

        hid = hid * lax.rsqrt(jnp.mean(hid * hid, axis=-1, keepdims=True) + RMS_EPS) * normw_ref[...]
        o_ref[0, :, h * ML_DV:(h + 1) * ML_DV] = (hid * _sigmoid(o_pre)).astype(o_ref.dtype)

    @pl.when(step == pl.num_programs(1) - 1)
    def _():
        co_ref[0] = c_ref[...]
        no_ref[0] = n_ref[...]
        mo_ref[0] = m_ref[:, 0:1]


def mlstm_group(proj, gates, gates_t, c0, n0, m0, in_slot, gate_b, norm_w, *, row0, nb, t, c,
                slot, n_slots, prev, tables=None, table_layer=0):
    tokens = proj.shape[0]
    nchunk = t // c
    blk0 = row0 // c
    nh = ML_HEADS
    proj3 = proj.reshape(tokens // c, c, proj.shape[1])
    ig = gates[:, :nh].reshape(tokens // c, c, nh)
    fg = gates[:, nh:].reshape(tokens // c, c, nh)
    ig_t = gates_t[:nh].reshape(nh, tokens // c, c).transpose(1, 0, 2)
    fg_t = gates_t[nh:].reshape(nh, tokens // c, c).transpose(1, 0, 2)

    def rows(b, s):
        return blk0 + b * nchunk + s

    kernel = functools.partial(_mlstm_kernel, c=c, cast=tables is not None)
    cast_args, cast_in, cast_out, cast_shape = _cast_specs(tables, table_layer, nb * nchunk,
                                                           lambda b, s: b * nchunk + s)
    prev_args, prev_specs, aliases = _alias_args(prev, first_input=13 + len(cast_args), first_output=1)
    o, cm, nv, m, *casts = pl.pallas_call(
        kernel,
        grid=(nb, nchunk),
        input_output_aliases=aliases,
        in_specs=[
            pl.BlockSpec((1, c, ML_MAIN_DIM), lambda b, s: (rows(b, s), 0, 0)),
            pl.BlockSpec((1, c, nh), lambda b, s: (rows(b, s), 0, 0)),
            pl.BlockSpec((1, c, nh), lambda b, s: (rows(b, s), 0, 0)),
            pl.BlockSpec((1, nh, c), lambda b, s: (rows(b, s), 0, 0)),
            pl.BlockSpec((1, nh, c), lambda b, s: (rows(b, s), 0, 0)),
            pl.BlockSpec((None, 1, nh, ML_DQK, ML_DV), lambda b, s: (in_slot, b, 0, 0, 0)),
            pl.BlockSpec((None, 1, nh, ML_DQK), lambda b, s: (in_slot, b, 0, 0)),
            pl.BlockSpec((None, 1, nh, 1), lambda b, s: (in_slot, b, 0, 0)),
            pl.BlockSpec((1, nh), lambda b, s: (0, 0)),
            pl.BlockSpec((1, nh), lambda b, s: (0, 0)),
            pl.BlockSpec((nh, 1), lambda b, s: (0, 0)),
            pl.BlockSpec((nh, 1), lambda b, s: (0, 0)),
            pl.BlockSpec((1, ML_DV), lambda b, s: (0, 0)),
        ] + cast_in + prev_specs,
        out_specs=[
            pl.BlockSpec((1, c, ML_V_DIM), lambda b, s: (b * nchunk + s, 0, 0)),
            pl.BlockSpec((None, 1, nh, ML_DQK, ML_DV), lambda b, s: (slot, b, 0, 0, 0)),
            pl.BlockSpec((None, 1, nh, ML_DQK), lambda b, s: (slot, b, 0, 0)),
            pl.BlockSpec((None, 1, nh, 1), lambda b, s: (slot, b, 0, 0)),
        ] + cast_out,
        out_shape=[
            jax.ShapeDtypeStruct((nb * nchunk, c, ML_V_DIM), BF16),
            jax.ShapeDtypeStruct((n_slots, nb, nh, ML_DQK, ML_DV), F32),
            jax.ShapeDtypeStruct((n_slots, nb, nh, ML_DQK), F32),
            jax.ShapeDtypeStruct((n_slots, nb, nh, 1), F32),
        ] + cast_shape,
        scratch_shapes=[pltpu.VMEM((nh, ML_DQK, ML_DV), F32),
                        pltpu.VMEM((nh, ML_DQK), F32),
                        pltpu.VMEM((nh, 128), F32),
                        pltpu.VMEM((nh, c, c), F32),
                        pltpu.VMEM((nh, c, ML_DV), F32)],
        compiler_params=_cparams("parallel", "arbitrary"),
        name=f"mlstm_chunk{c}",
    )(proj3, ig, fg, ig_t, fg_t, c0, n0, m0.reshape(m0.shape + (1,)),
      gate_b[:nh].reshape(1, nh), gate_b[nh:].reshape(1, nh),
      gate_b[:nh].reshape(nh, 1), gate_b[nh:].reshape(nh, 1), norm_w.reshape(1, ML_DV), *cast_args, *prev_args)
    return o.reshape(nb * t, ML_V_DIM), (cm, nv, m), casts


_CAND_PAIRS = [(a, b) for a in range(PEER_TOPK) for b in range(PEER_TOPK) if (a + 1) * (b + 1) <= PEER_TOPK]
_CAND_ROWS = -(-len(_CAND_PAIRS) // 8) * 8


def _top_values(work, count):
    rows = lax.broadcasted_iota(jnp.int32, work.shape, 0)
    out = []
    for r in range(count):
        m = jnp.max(work, axis=0, keepdims=True)
        out.append(m)
        if r + 1 < count:
            first = jnp.min(jnp.where(work == m, rows, work.shape[0]), axis=0, keepdims=True)
            work = jnp.where(rows == first, NEG_INF, work)
    return out


def _sorting_network(n):
    pairs = []
    p = 1
    while p < n:
        k = p
        while k >= 1:
            for j in range(k % p, n - k, 2 * k):
                for i in range(min(k, n - j - k)):
                    if (i + j) // (2 * p) == (i + j + k) // (2 * p):
                        pairs.append((i + j, i + j + k))
            k //= 2
        p *= 2
    return pairs


def _compare_exchange(vals, i, j):
    vals[i], vals[j] = jnp.maximum(vals[i], vals[j]), jnp.minimum(vals[i], vals[j])


def _top16_of_128(s):
    k = PEER_TOPK
    slab = s.shape[0] // k
    vals = [s[i * slab:(i + 1) * slab, :] for i in range(k)]
    for i, j in _sorting_network(k):
        _compare_exchange(vals, i, j)
    shift = slab // 2
    while shift >= 1:
        vals = [jnp.maximum(vals[i], pltpu.roll(vals[k - 1 - i], shift, 0)) for i in range(k)]
        stride = k // 2
        while stride >= 1:
            for i in range(k):
                if (i // stride) % 2 == 0:
                    _compare_exchange(vals, i, i + stride)
            stride //= 2
        shift //= 2
    return [v[0:1, :] for v in vals]


def _peer_route_kernel(q_ref, keys_ref, th1_ref, w1_ref, s2_ref, w2_ref, cand_ref):
    cand_ref[...] = jnp.full(cand_ref.shape, NEG_INF, F32)
    for h in range(PEER_HEADS):
        scores, tops = [], []
        for p in range(2):
            col = (2 * h + p) * PEER_HALF
            s = _dot_nt(keys_ref[h, p], q_ref[:, col:col + PEER_HALF])
            scores.append(s)
            tops.append(_top16_of_128(s))
        for i, (a, b) in enumerate(_CAND_PAIRS):
            cand_ref[i:i + 1, :] = tops[0][a] + tops[1][b]
        cand = cand_ref[...]
        tau = _top_values(cand, PEER_TOPK)[-1]
        max1, max2 = tops[0][0], tops[1][0]
        z = jnp.sum(jnp.where(cand >= tau, jnp.exp(cand - (max1 + max2)), 0.0), axis=0, keepdims=True)
        cut = jnp.full(scores[0].shape, POS_INF, F32)
        for a in range(PEER_TOPK):
            cut_a = jnp.full(tau.shape, POS_INF, F32)
            for b in range(PEER_TOPK // (a + 1)):
                cut_a = jnp.where(tops[0][a] + tops[1][b] >= tau, tops[1][b], cut_a)
            cut = jnp.where(scores[0] == tops[0][a], cut_a, cut)
        th1_ref[h] = cut
        w1_ref[h] = jnp.exp(scores[0] - max1)
        s2_ref[h] = scores[1]
        w2_ref[h] = jnp.exp(scores[1] - max2) / z


def peer_route(q, keys, *, tm):
    m = q.shape[0]
    nh = PEER_HEADS
    table = jax.ShapeDtypeStruct((nh, PEER_N_KEYS, m), F32)
    table_spec = pl.BlockSpec((nh, PEER_N_KEYS, tm), lambda i: (0, 0, i))
    return pl.pallas_call(
        _peer_route_kernel,
        grid=(m // tm,),
        in_specs=[pl.BlockSpec((tm, q.shape[1]), lambda i: (i, 0)),
                  pl.BlockSpec(keys.shape, lambda i: (0, 0, 0, 0))],
        out_specs=[table_spec] * 4,
        out_shape=[table] * 4,
        scratch_shapes=[pltpu.VMEM((_CAND_ROWS, tm), F32)],
        compiler_params=_cparams("parallel"),
        name="peer_route",
    )(q, keys)


def _peer_main_kernel(xb_ref, u_ref, vt_ref, th1_ref, w1_ref, s2_ref, w2_ref, x_ref, g_ref, b_ref,
                      o_ref, ob_ref, acc_ref, ht_ref, a_ref, *, te):
    e = pl.program_id(1)
    groups = te // PEER_N_KEYS
    tm = ht_ref.shape[1]

    @pl.when(e == 0)
    def _():
        acc_ref[...] = jnp.zeros_like(acc_ref)

    ht_ref[...] = _dot_nt(u_ref[...], xb_ref[...])
    for cc in range(groups):
        key1 = e * groups + cc
        rows = slice(cc * PEER_N_KEYS, (cc + 1) * PEER_N_KEYS)
        for t0 in range(0, tm, 2 * LANE_TILE):
            pair = slice(t0, t0 + 2 * LANE_TILE)
            th_rows = [th1_ref[h, pl.ds(key1, 1), pair] for h in range(PEER_HEADS)]
            w1_rows = [w1_ref[h, pl.ds(key1, 1), pair] for h in range(PEER_HEADS)]
            for half in range(2):
                lanes = slice(t0 + half * LANE_TILE, t0 + (half + 1) * LANE_TILE)
                sub = slice(half * LANE_TILE, (half + 1) * LANE_TILE)
                gate = jnp.zeros((PEER_N_KEYS, LANE_TILE), F32)
                for h in range(PEER_HEADS):
                    hit = s2_ref[h, :, lanes] >= th_rows[h][:, sub]
                    gate = gate + jnp.where(hit, w2_ref[h, :, lanes], 0.0) * w1_rows[h][:, sub]
                a_ref[rows, lanes] = (_gelu_tanh(ht_ref[rows, lanes]) * gate).astype(BF16)
    acc_ref[...] += _dot(vt_ref[...], a_ref[...])

    @pl.when(e == pl.num_programs(1) - 1)
    def _():
        for t0 in range(0, tm, LANE_TILE):
            tok = slice(t0, t0 + LANE_TILE)
            y = _layer_norm_rows(DEEPNORM_ALPHA * x_ref[tok, :] + acc_ref[:, tok].T, g_ref[...], b_ref[...])
            o_ref[tok, :] = y
            ob_ref[tok, :] = y.astype(BF16)


def peer_main(xb, x, u, vt, th1, w1, s2, w2, g, b, *, tm, te):
    m, d = x.shape
    ne = u.shape[0]
    nh = PEER_HEADS
    kernel = functools.partial(_peer_main_kernel, te=te)
    once = pl.Buffered(1)
    table_spec = pl.BlockSpec((nh, PEER_N_KEYS, tm), lambda i, e: (0, 0, i), pipeline_mode=once)
    return pl.pallas_call(
        kernel,
        grid=(m // tm, ne // te),
        in_specs=[pl.BlockSpec((tm, d), lambda i, e: (i, 0), pipeline_mode=once),
                  pl.BlockSpec((te, d), lambda i, e: (e, 0)),
                  pl.BlockSpec((d, te), lambda i, e: (0, e)),
                  table_spec, table_spec, table_spec, table_spec,
                  pl.BlockSpec((tm, d), lambda i, e: (i, 0), pipeline_mode=once),
                  pl.BlockSpec((1, d), lambda i, e: (0, 0)),
                  pl.BlockSpec((1, d), lambda i, e: (0, 0))],
        out_specs=[pl.BlockSpec((tm, d), lambda i, e: (i, 0), pipeline_mode=once),
                   pl.BlockSpec((tm, d), lambda i, e: (i, 0), pipeline_mode=once)],
        out_shape=[jax.ShapeDtypeStruct((m, d), F32), jax.ShapeDtypeStruct((m, d), BF16)],
        scratch_shapes=[pltpu.VMEM((d, tm), F32), pltpu.VMEM((te, tm), F32), pltpu.VMEM((te, tm), BF16)],
        compiler_params=_cparams("parallel", "arbitrary"),
        name="peer_main",
    )(xb, u, vt, th1, w1, s2, w2, x, g.reshape(1, d), b.reshape(1, d))


TOKEN_TILE = 3072
GATE_TOKEN_TILE = 1024
PROJ_COL_TILE = 512
OUT_COL_TILE = 512
OUT_ROW_TILE = 512
PEER_TOKEN_TILE = 512
PEER_EXPERT_TILE = 1024


def kernel(x_prompt, x_sample, state_gdn_conv, state_gdn_s, state_mlstm_c, state_mlstm_n, state_mlstm_m,
           gdn_w_in, gdn_conv_w, gdn_a_log, gdn_dt_bias, gdn_norm_w, gdn_w_out,
           ml_w_in, ml_gate_b, ml_norm_w, ml_w_out,
           ln_mix_g, ln_mix_b, ln_ffn_g, ln_ffn_b,
           peer_w_q, peer_keys, peer_u, peer_v):
    bp, tp, d = x_prompt.shape
    bs, ts, _ = x_sample.shape
    np_tok = bp * tp
    ns_tok = bs * ts
    x = jnp.concatenate([x_prompt.reshape(np_tok, d), x_sample.reshape(ns_tok, d)], axis=0)
    xb = x.astype(BF16)
    cp = min(GDN_CHUNK, tp)
    cs = min(GDN_CHUNK, ts)

    n_gdn = state_gdn_s.shape[0]
    n_ml = state_mlstm_c.shape[0]
    zero_gdn = (jnp.zeros((1, bp) + state_gdn_conv.shape[2:], F32), jnp.zeros((1, bp) + state_gdn_s.shape[2:], F32))
    zero_ml = (jnp.zeros((1, bp) + state_mlstm_c.shape[2:], F32), jnp.zeros((1, bp) + state_mlstm_n.shape[2:], F32),
               jnp.zeros((1, bp) + state_mlstm_m.shape[2:], F32))
    p_gdn = s_gdn = p_ml = s_ml = None
    for layer in range(DEPTH):
        j = layer // 2
        prompt = dict(row0=0, nb=bp, t=tp, c=cp, slot=j)
        sample = dict(row0=np_tok, nb=bs, t=ts, c=cs, slot=j)
        if layer % 2 == 0:
            proj = matmul(xb, gdn_w_in, j, GDN_MAIN_DIM, tm=TOKEN_TILE, tn=PROJ_COL_TILE)
            gates, gates_t = gate_proj(xb, gdn_w_in[j, :, GDN_MAIN_DIM:], tm=GATE_TOKEN_TILE)
            shared = (gdn_conv_w[j], gdn_a_log[j], gdn_dt_bias[j], gdn_norm_w[j])
            o_p, p_gdn, (u_bf16, vt_bf16) = gdn_group(proj, gates, gates_t, *zero_gdn, 0, *shared, n_slots=n_gdn,
                                                      prev=p_gdn, tables=(peer_u, peer_v), table_layer=layer, **prompt)
            o_s, s_gdn, _ = gdn_group(proj, gates, gates_t, state_gdn_conv, state_gdn_s, j, *shared,
                                      n_slots=n_gdn, prev=s_gdn, **sample)
            w_out = gdn_w_out
        else:
            proj = matmul(xb, ml_w_in, j, ML_MAIN_DIM, tm=TOKEN_TILE, tn=PROJ_COL_TILE)
            gates, gates_t = gate_proj(xb, ml_w_in[j, :, ML_MAIN_DIM:], tm=GATE_TOKEN_TILE)
            shared = (ml_gate_b[j], ml_norm_w[j])
            o_p, p_ml, (u_bf16, vt_bf16) = mlstm_group(proj, gates, gates_t, *zero_ml, 0, *shared, n_slots=n_ml,
                                                       prev=p_ml, tables=(peer_u, peer_v), table_layer=layer, **prompt)
            o_s, s_ml, _ = mlstm_group(proj, gates, gates_t, state_mlstm_c, state_mlstm_n, state_mlstm_m, j, *shared,
                                       n_slots=n_ml, prev=s_ml, **sample)
            w_out = ml_w_out
        x, xb = out_proj_ln(o_p, o_s, w_out, j, x, ln_mix_g[layer], ln_mix_b[layer],
                            tm=OUT_ROW_TILE, tn=OUT_COL_TILE)

        q = matmul(xb, peer_w_q, layer, peer_w_q.shape[2], tm=TOKEN_TILE, tn=PROJ_COL_TILE)
        tables = peer_route(q, peer_keys[layer], tm=PEER_TOKEN_TILE)
        x, xb = peer_main(xb, x, u_bf16, vt_bf16, *tables,
                          ln_ffn_g[layer], ln_ffn_b[layer], tm=PEER_TOKEN_TILE, te=PEER_EXPERT_TILE)

    y_prompt = x[:np_tok].reshape(bp, tp, d)
    y_sample = x[np_tok:].reshape(bs, ts, d)
    return (y_prompt, y_sample,
            p_gdn[0], p_gdn[1], p_ml[0], p_ml[1], p_ml[2].reshape(n_ml, bp, ML_HEADS),
            s_gdn[0], s_gdn[1], s_ml[0], s_ml[1], s_ml[2].reshape(n_ml, bs, ML_HEADS))
```

```python
import functools

import jax
import jax.numpy as jnp
from jax import lax
from jax.experimental import pallas as pl
from jax.experimental.pallas import tpu as pltpu

F32 = jnp.float32
BF16 = jnp.bfloat16

D_MODEL = 2048
DEPTH = 4
GDN_K_HEADS = 16
GDN_V_HEADS = 32
GDN_DK = 128
GDN_DV = 128
GDN_QK_DIM = GDN_K_HEADS * GDN_DK
GDN_V_DIM = GDN_V_HEADS * GDN_DV
GDN_CONV_DIM = 2 * GDN_QK_DIM + GDN_V_DIM
GDN_MAIN_DIM = GDN_CONV_DIM + GDN_V_DIM
GDN_CONV_W = 4
GDN_CHUNK = 64
ML_HEADS = 8
ML_DQK = 128
ML_DV = 256
ML_QK_DIM = ML_HEADS * ML_DQK
ML_V_DIM = ML_HEADS * ML_DV
ML_MAIN_DIM = 2 * ML_QK_DIM + 2 * ML_V_DIM
ML_CHUNK = 64
PEER_HEADS = 8
PEER_N_KEYS = 128
PEER_HALF = 128
PEER_TOPK = 16
DEEPNORM_ALPHA = (2 * DEPTH) ** 0.25
LN_EPS = 1e-5
RMS_EPS = 1e-6
NEG_BIG = -1e30
NEG_INF = float("-inf")
POS_INF = float("inf")

VMEM_LIMIT_BYTES = 56 * 1024 * 1024
LANE_TILE = 128
HIGHEST = lax.Precision.HIGHEST


def _cparams(*sem):
    return pltpu.CompilerParams(dimension_semantics=sem, vmem_limit_bytes=VMEM_LIMIT_BYTES)


def _dot(a, b):
    return jnp.dot(a.astype(BF16), b.astype(BF16), preferred_element_type=F32)


def _dot_nt(a, b):
    return lax.dot_general(a.astype(BF16), b.astype(BF16), (((1,), (1,)), ((), ())),
                           preferred_element_type=F32)


def _dot_tn(a, b):
    return lax.dot_general(a.astype(BF16), b.astype(BF16), (((0,), (0,)), ((), ())),
                           preferred_element_type=F32)


def _dot_f32(a, b):
    return jnp.dot(a, b, preferred_element_type=F32, precision=HIGHEST)


def _sigmoid(x):
    return 1.0 / (1.0 + jnp.exp(-x))


def _silu(x):
    return x * _sigmoid(x)


def _softplus(x):
    return jnp.maximum(x, 0.0) + jnp.log(1.0 + jnp.exp(-jnp.abs(x)))


def _log_sigmoid(x):
    return -_softplus(-x)


def _gelu_tanh(x):
    return 0.5 * x * (1.0 + jnp.tanh(0.7978845608028654 * (x + 0.044715 * (x * x * x))))


def _lower(c, strict):
    r = lax.broadcasted_iota(jnp.int32, (c, c), 0)
    k = lax.broadcasted_iota(jnp.int32, (c, c), 1)
    return (r > k) if strict else (r >= k)


def _upper(c):
    r = lax.broadcasted_iota(jnp.int32, (c, c), 0)
    k = lax.broadcasted_iota(jnp.int32, (c, c), 1)
    return r <= k


def _layer_norm_rows(v, g, b):
    mu = jnp.mean(v, axis=-1, keepdims=True)
    d = v - mu
    var = jnp.mean(d * d, axis=-1, keepdims=True)
    return d * lax.rsqrt(var + LN_EPS) * g + b


def _matmul_kernel(x_ref, w_ref, o_ref):
    o_ref[...] = _dot(x_ref[...], w_ref[...])


def matmul(x, w_stack, layer, n_cols, *, tm, tn):
    m, k = x.shape
    return pl.pallas_call(
        _matmul_kernel,
        grid=(m // tm, n_cols // tn),
        in_specs=[pl.BlockSpec((tm, k), lambda i, j: (i, 0), pipeline_mode=pl.Buffered(1)),
                  pl.BlockSpec((None, k, tn), lambda i, j: (layer, 0, j))],
        out_specs=pl.BlockSpec((tm, tn), lambda i, j: (i, j)),
        out_shape=jax.ShapeDtypeStruct((m, n_cols), F32),
        compiler_params=_cparams("parallel", "arbitrary"),
        name="proj_matmul",
    )(x, w_stack)


def _gate_proj_kernel(x_ref, w_ref, wt_ref, o_ref, ot_ref):
    x = x_ref[...]
    o_ref[...] = _dot(x, w_ref[...])
    ot_ref[...] = _dot_nt(wt_ref[...], x)


def gate_proj(x, w_gate, *, tm):
    m, k = x.shape
    n = w_gate.shape[1]
    return pl.pallas_call(
        _gate_proj_kernel,
        grid=(m // tm,),
        in_specs=[pl.BlockSpec((tm, k), lambda i: (i, 0)),
                  pl.BlockSpec((k, n), lambda i: (0, 0)),
                  pl.BlockSpec((n, k), lambda i: (0, 0))],
        out_specs=[pl.BlockSpec((tm, n), lambda i: (i, 0)),
                   pl.BlockSpec((n, tm), lambda i: (0, i))],
        out_shape=[jax.ShapeDtypeStruct((m, n), F32), jax.ShapeDtypeStruct((n, m), F32)],
        compiler_params=_cparams("parallel"),
        name="gate_proj",
    )(x, w_gate, w_gate.T)


def _out_proj_ln_kernel(hp_ref, hs_ref, w_ref, x_ref, g_ref, b_ref, o_ref, ob_ref, acc_ref, *, prompt_tiles):
    i = pl.program_id(0)
    kk = pl.program_id(1)

    @pl.when(kk == 0)
    def _():
        acc_ref[...] = jnp.zeros_like(acc_ref)

    @pl.when(i < prompt_tiles)
    def _():
        acc_ref[...] += _dot(hp_ref[...], w_ref[...])

    @pl.when(i >= prompt_tiles)
    def _():
        acc_ref[...] += _dot(hs_ref[...], w_ref[...])

    @pl.when(kk == pl.num_programs(1) - 1)
    def _():
        y = _layer_norm_rows(DEEPNORM_ALPHA * x_ref[...] + acc_ref[...], g_ref[...], b_ref[...])
        o_ref[...] = y
        ob_ref[...] = y.astype(BF16)


def out_proj_ln(h_prompt, h_sample, w_stack, layer, x, g, b, *, tm, tk):
    mp, k = h_prompt.shape
    ms = h_sample.shape[0]
    d = w_stack.shape[2]
    prompt_tiles = mp // tm
    last_k = k // tk - 1
    kernel = functools.partial(_out_proj_ln_kernel, prompt_tiles=prompt_tiles)
    return pl.pallas_call(
        kernel,
        grid=((mp + ms) // tm, k // tk),
        in_specs=[pl.BlockSpec((tm, tk), lambda i, j: (jnp.minimum(i, prompt_tiles - 1),
                                                        jnp.where(i < prompt_tiles, j, last_k))),
                  pl.BlockSpec((tm, tk), lambda i, j: (jnp.maximum(i - prompt_tiles, 0),
                                                        jnp.where(i < prompt_tiles, 0, j))),
                  pl.BlockSpec((None, tk, d), lambda i, j: (layer, j, 0)),
                  pl.BlockSpec((tm, d), lambda i, j: (i, 0), pipeline_mode=pl.Buffered(1)),
                  pl.BlockSpec((1, d), lambda i, j: (0, 0)),
                  pl.BlockSpec((1, d), lambda i, j: (0, 0))],
        out_specs=[pl.BlockSpec((tm, d), lambda i, j: (i, 0), pipeline_mode=pl.Buffered(1)),
                   pl.BlockSpec((tm, d), lambda i, j: (i, 0), pipeline_mode=pl.Buffered(1))],
        out_shape=[jax.ShapeDtypeStruct((mp + ms, d), F32), jax.ShapeDtypeStruct((mp + ms, d), BF16)],
        scratch_shapes=[pltpu.VMEM((tm, d), F32)],
        compiler_params=_cparams("parallel", "arbitrary"),
        name="out_proj_ln",
    )(h_prompt, h_sample, w_stack, x, g.reshape(1, d), b.reshape(1, d))


CONV_PAD = 8


def _dot_split(e, r):
    c = e.shape[0]
    e_hi = e.astype(BF16).astype(F32)
    r_hi = r.astype(BF16).astype(F32)
    stacked = _dot(jnp.concatenate([e_hi, e - e_hi], axis=0), r_hi)
    return stacked[0:c] + stacked[c:2 * c] + _dot(e_hi, r - r_hi)


def _cast_table_slab(u_ref, v_ref, ub_ref, vt_ref):
    ub_ref[...] = u_ref[...].astype(BF16)
    vt_ref[...] = v_ref[...].T.astype(BF16)


def _cast_specs(tables, layer, steps, step_index):
    if tables is None:
        return [], [], [], []
    ne, d = tables[0].shape[1:]
    slab = ne // steps
    in_specs = [pl.BlockSpec((None, slab, d), lambda b, s: (layer, step_index(b, s), 0))] * 2
    out_specs = [pl.BlockSpec((slab, d), lambda b, s: (step_index(b, s), 0)),
                 pl.BlockSpec((d, slab), lambda b, s: (0, step_index(b, s)))]
    out_shape = [jax.ShapeDtypeStruct((ne, d), BF16), jax.ShapeDtypeStruct((d, ne), BF16)]
    return list(tables), in_specs, out_specs, out_shape


def _gdn_kernel(qkv_ref, z_ref, bpre_ref, apre_ref, bpre_t_ref, apre_t_ref, conv0_ref, s0_ref,
                convw_ref, alog_ref, dtb_ref, alog_t_ref, dtb_t_ref, normw_ref, *rest, c, cast):
    (xp_ref, s_ref, kq_ref, l_ref, a_ref, e_ref, p_ref, rhs_ref, lhs_ref, kd_ref, vn_ref) = rest[-11:]
    if cast:
        _cast_table_slab(rest[0], rest[1], rest[-13], rest[-12])
        o_ref, convo_ref, so_ref = rest[-16:-13]
    else:
        o_ref, convo_ref, so_ref = rest[-14:-11]
    step = pl.program_id(1)
    hist = GDN_CONV_W - 1
    heads = range(GDN_V_HEADS)
    rep = GDN_V_HEADS // GDN_K_HEADS
    tok = slice(CONV_PAD, CONV_PAD + c)

    @pl.when(step == 0)
    def _():
        xp_ref[CONV_PAD - hist:CONV_PAD, :] = conv0_ref[0]
        s_ref[...] = s0_ref[0]

    x = qkv_ref[0]
    xp_ref[tok, :] = x
    y = xp_ref[CONV_PAD - hist:CONV_PAD - hist + c, :] * convw_ref[0:1, :]
    for i in range(1, GDN_CONV_W):
        y = y + xp_ref[CONV_PAD - hist + i:CONV_PAD - hist + i + c, :] * convw_ref[i:i + 1, :]
    tail = x[c - hist:c, :]
    xp_ref[CONV_PAD - hist:CONV_PAD, :] = tail
    convo_ref[0] = tail
    xp_ref[tok, :] = _silu(y)

    beta_c = _sigmoid(bpre_ref[0])
    g_c = -jnp.exp(alog_ref[...]) * _softplus(apre_ref[0] + dtb_ref[...])
    g_r = -jnp.exp(alog_t_ref[...]) * _softplus(apre_t_ref[0] + dtb_t_ref[...])
    rows = lax.broadcasted_iota(jnp.int32, (c, c), 0)
    cols = lax.broadcasted_iota(jnp.int32, (c, c), 1)
    causal = rows >= cols
    strict = rows > cols
    gc_c = _dot_f32(causal.astype(F32), g_c)
    gc_r = _dot_f32(g_r, (rows <= cols).astype(F32))

    for kh in range(GDN_K_HEADS):
        qs = slice(kh * GDN_DK, (kh + 1) * GDN_DK)
        ks = slice(GDN_QK_DIM + kh * GDN_DK, GDN_QK_DIM + (kh + 1) * GDN_DK)
        q = xp_ref[tok, qs]
        k = xp_ref[tok, ks]
        q = q * lax.rsqrt(jnp.sum(q * q, axis=-1, keepdims=True) + RMS_EPS) * (GDN_DK ** -0.5)
        k = k * lax.rsqrt(jnp.sum(k * k, axis=-1, keepdims=True) + RMS_EPS)
        xp_ref[tok, qs] = q
        xp_ref[tok, ks] = k
        kq_ref[kh] = _dot_nt(jnp.concatenate([k, q], axis=0), k)

    diag8 = strict & ((rows >> 3) == (cols >> 3))
    for h in heads:
        kh = h // rep
        q = xp_ref[tok, kh * GDN_DK:(kh + 1) * GDN_DK]
        k = xp_ref[tok, GDN_QK_DIM + kh * GDN_DK:GDN_QK_DIM + (kh + 1) * GDN_DK]
        v = xp_ref[tok, 2 * GDN_QK_DIM + h * GDN_DV:2 * GDN_QK_DIM + (h + 1) * GDN_DV]
        gcol = gc_c[:, h:h + 1]
        bcol = beta_c[:, h:h + 1]
        diff = gcol - gc_r[h:h + 1, :]
        egc = jnp.exp(gcol)
        l = bcol * kq_ref[kh, 0:c, :] * jnp.exp(jnp.where(strict, diff, NEG_BIG))
        l_ref[h] = l
        a_ref[h] = kq_ref[kh, c:2 * c, :] * jnp.exp(jnp.where(causal, diff, NEG_BIG))
        rhs_ref[h, :, 0:GDN_DV] = v * bcol
        rhs_ref[h, :, GDN_DV:GDN_DV + GDN_DK] = k * (bcol * egc)
        lhs_ref[h, c:2 * c, :] = q * egc
        kd_ref[h] = k * jnp.exp(gcol[c - 1:c, :] - gcol)
        l8 = jnp.where(diag8, l, 0.0)
        e_ref[h] = -l8
        p_ref[h] = _dot(l8, l8)
    for h in heads:
        e = e_ref[h]
        p = p_ref[h]
        e_ref[h] = e + p + _dot(e, p)
        p_ref[h] = _dot(p, p)
    for h in heads:
        e = e_ref[h]
        p = p_ref[h]
        e_ref[h] = e + p + _dot(e, p)
    shift = 3
    while (2 << shift) <= c:
        lower_left = (((rows >> (shift + 1)) == (cols >> (shift + 1)))
                      & (((rows >> shift) & 1) == 1) & (((cols >> shift) & 1) == 0))
        for h in heads:
            n = jnp.where(lower_left, l_ref[h], 0.0)
            p_ref[h] = n + _dot(e_ref[h], n)
        for h in heads:
            t = p_ref[h]
            e = e_ref[h]
            e_ref[h] = e - t - _dot(t, e)
        shift += 1

    for h in heads:
        r = rhs_ref[h]
        uw = r + _dot_split(e_ref[h], r)
        rhs_ref[h, :, 0:GDN_DV] = uw[:, 0:GDN_DV]
        lhs_ref[h, 0:c, :] = uw[:, GDN_DV:GDN_DV + GDN_DK]
    for h in heads:
        ws = _dot(lhs_ref[h], s_ref[h])
        vn_ref[h] = rhs_ref[h, :, 0:GDN_DV] - ws[0:c]
        lhs_ref[h, c:2 * c, :] = ws[c:2 * c]
    for h in heads:
        v_new = vn_ref[h]
        o = lhs_ref[h, c:2 * c, :] + _dot(a_ref[h], v_new)
        s_ref[h] = s_ref[h] * jnp.exp(gc_c[c - 1:c, h:h + 1]) + _dot_tn(kd_ref[h], v_new)
        o = o * lax.rsqrt(jnp.mean(o * o, axis=-1, keepdims=True) + RMS_EPS) * normw_ref[...]
        zh = z_ref[0, :, h * GDN_DV:(h + 1) * GDN_DV]
        o_ref[0, :, h * GDN_DV:(h + 1) * GDN_DV] = (o * _silu(zh)).astype(o_ref.dtype)

    @pl.when(step == pl.num_programs(1) - 1)
    def _():
        so_ref[0] = s_ref[...]


def _alias_args(prev, first_input, first_output):
    if prev is None:
        return [], [], {}
    specs = [pl.BlockSpec(memory_space=pl.ANY)] * len(prev)
    aliases = {first_input + i: first_output + i for i in range(len(prev))}
    return list(prev), specs, aliases


def gdn_group(proj, gates, gates_t, conv0, s0, in_slot, conv_w, a_log, dt_bias, norm_w, *, row0, nb, t, c,
              slot, n_slots, prev, tables=None, table_layer=0):
    tokens = proj.shape[0]
    nchunk = t // c
    blk0 = row0 // c
    hv = GDN_V_HEADS
    proj3 = proj.reshape(tokens // c, c, proj.shape[1])
    bpre = gates[:, :hv].reshape(tokens // c, c, hv)
    apre = gates[:, hv:].reshape(tokens // c, c, hv)
    bpre_t = gates_t[:hv].reshape(hv, tokens // c, c).transpose(1, 0, 2)
    apre_t = gates_t[hv:].reshape(hv, tokens // c, c).transpose(1, 0, 2)
    z_blk = GDN_CONV_DIM // GDN_V_DIM

    def rows(b, s):
        return blk0 + b * nchunk + s

    kernel = functools.partial(_gdn_kernel, c=c, cast=tables is not None)
    cast_args, cast_in, cast_out, cast_shape = _cast_specs(tables, table_layer, nb * nchunk,
                                                           lambda b, s: b * nchunk + s)
    prev_args, prev_specs, aliases = _alias_args(prev, first_input=14 + len(cast_args), first_output=1)
    o, conv, s, *casts = pl.pallas_call(
        kernel,
        grid=(nb, nchunk),
        input_output_aliases=aliases,
        in_specs=[
            pl.BlockSpec((1, c, GDN_CONV_DIM), lambda b, s: (rows(b, s), 0, 0)),
            pl.BlockSpec((1, c, GDN_V_DIM), lambda b, s: (rows(b, s), 0, z_blk)),
            pl.BlockSpec((1, c, hv), lambda b, s: (rows(b, s), 0, 0)),
            pl.BlockSpec((1, c, hv), lambda b, s: (rows(b, s), 0, 0)),
            pl.BlockSpec((1, hv, c), lambda b, s: (rows(b, s), 0, 0)),
            pl.BlockSpec((1, hv, c), lambda b, s: (rows(b, s), 0, 0)),
            pl.BlockSpec((None, 1, GDN_CONV_W - 1, GDN_CONV_DIM), lambda b, s: (in_slot, b, 0, 0)),
            pl.BlockSpec((None, 1, hv, GDN_DK, GDN_DV), lambda b, s: (in_slot, b, 0, 0, 0)),
            pl.BlockSpec((GDN_CONV_W, GDN_CONV_DIM), lambda b, s: (0, 0)),
            pl.BlockSpec((1, hv), lambda b, s: (0, 0)),
            pl.BlockSpec((1, hv), lambda b, s: (0, 0)),
            pl.BlockSpec((hv, 1), lambda b, s: (0, 0)),
            pl.BlockSpec((hv, 1), lambda b, s: (0, 0)),
            pl.BlockSpec((1, GDN_DV), lambda b, s: (0, 0)),
        ] + cast_in + prev_specs,
        out_specs=[
            pl.BlockSpec((1, c, GDN_V_DIM), lambda b, s: (b * nchunk + s, 0, 0)),
            pl.BlockSpec((None, 1, GDN_CONV_W - 1, GDN_CONV_DIM), lambda b, s: (slot, b, 0, 0)),
            pl.BlockSpec((None, 1, hv, GDN_DK, GDN_DV), lambda b, s: (slot, b, 0, 0, 0)),
        ] + cast_out,
        out_shape=[
            jax.ShapeDtypeStruct((nb * nchunk, c, GDN_V_DIM), BF16),
            jax.ShapeDtypeStruct((n_slots, nb, GDN_CONV_W - 1, GDN_CONV_DIM), F32),
            jax.ShapeDtypeStruct((n_slots, nb, hv, GDN_DK, GDN_DV), F32),
        ] + cast_shape,
        scratch_shapes=[pltpu.VMEM((CONV_PAD + c, GDN_CONV_DIM), F32),
                        pltpu.VMEM((hv, GDN_DK, GDN_DV), F32),
                        pltpu.VMEM((GDN_K_HEADS, 2 * c, c), F32),
                        pltpu.VMEM((hv, c, c), F32),
                        pltpu.VMEM((hv, c, c), F32),
                        pltpu.VMEM((hv, c, c), F32),
                        pltpu.VMEM((hv, c, c), F32),
                        pltpu.VMEM((hv, c, GDN_DV + GDN_DK), F32),
                        pltpu.VMEM((hv, 2 * c, GDN_DK), F32),
                        pltpu.VMEM((hv, c, GDN_DK), F32),
                        pltpu.VMEM((hv, c, GDN_DV), F32)],
        compiler_params=_cparams("parallel", "arbitrary"),
        name=f"gdn_chunk{c}",
    )(proj3, proj3, bpre, apre, bpre_t, apre_t, conv0, s0, conv_w,
      a_log.reshape(1, hv), dt_bias.reshape(1, hv), a_log.reshape(hv, 1), dt_bias.reshape(hv, 1),
      norm_w.reshape(1, GDN_DV), *cast_args, *prev_args)
    return o.reshape(nb * t, GDN_V_DIM), (conv, s), casts


def _mlstm_kernel(main_ref, ig_ref, fg_ref, ig_t_ref, fg_t_ref, c0_ref, n0_ref, m0_ref,
                  bi_ref, bf_ref, bi_t_ref, bf_t_ref, normw_ref, *rest, c, cast):
    c_ref, n_ref, m_ref, qk_ref, qc_ref = rest[-5:]
    if cast:
        _cast_table_slab(rest[0], rest[1], rest[-7], rest[-6])
        o_ref, co_ref, no_ref, mo_ref = rest[-11:-7]
    else:
        o_ref, co_ref, no_ref, mo_ref = rest[-9:-5]
    step = pl.program_id(1)

    @pl.when(step == 0)
    def _():
        c_ref[...] = c0_ref[0]
        n_ref[...] = n0_ref[0]
        m_ref[...] = jnp.broadcast_to(m0_ref[0], m_ref.shape)

    ig_c = ig_ref[0] + bi_ref[...]
    lf_c = _log_sigmoid(fg_ref[0] + bf_ref[...])
    ig_r = ig_t_ref[0] + bi_t_ref[...]
    lf_r = _log_sigmoid(fg_t_ref[0] + bf_t_ref[...])
    causal = _lower(c, strict=False)
    bc_c = _dot_f32(causal.astype(F32), lf_c)
    bc_r = _dot_f32(lf_r, _upper(c).astype(F32))

    for h in range(ML_HEADS):
        q = main_ref[0, :, h * ML_DQK:(h + 1) * ML_DQK]
        k = main_ref[0, :, ML_QK_DIM + h * ML_DQK:ML_QK_DIM + (h + 1) * ML_DQK] * (ML_DQK ** -0.5)
        qk_ref[h] = _dot_nt(q, k)
        qc_ref[h] = _dot(q, c_ref[h])

    for h in range(ML_HEADS):
        q = main_ref[0, :, h * ML_DQK:(h + 1) * ML_DQK]
        k = main_ref[0, :, ML_QK_DIM + h * ML_DQK:ML_QK_DIM + (h + 1) * ML_DQK] * (ML_DQK ** -0.5)
        v = main_ref[0, :, 2 * ML_QK_DIM + h * ML_DV:2 * ML_QK_DIM + (h + 1) * ML_DV]
        o_pre = main_ref[0, :, 2 * ML_QK_DIM + ML_V_DIM + h * ML_DV:2 * ML_QK_DIM + ML_V_DIM + (h + 1) * ML_DV]
        bcol = bc_c[:, h:h + 1]
        brow = bc_r[h:h + 1, :]
        icol = ig_c[:, h:h + 1]
        irow = ig_r[h:h + 1, :]
        m_prev = m_ref[h:h + 1, 0:1]
        cm = c_ref[h]
        nv = n_ref[h:h + 1, :]

        d = jnp.where(causal, bcol - brow + irow, NEG_BIG)
        inter = bcol + m_prev
        m_t = jnp.maximum(inter, jnp.max(d, axis=-1, keepdims=True))
        w_intra = jnp.exp(d - m_t)
        w_inter = jnp.exp(inter - m_t)
        sqk = qk_ref[h] * w_intra
        num = w_inter * qc_ref[h] + _dot(sqk, v)
        den = w_inter * jnp.sum(q * nv, axis=-1, keepdims=True) + jnp.sum(sqk, axis=-1, keepdims=True)
        hid = num / jnp.maximum(jnp.abs(den), jnp.exp(-m_t))

        b_last = bcol[c - 1:c, :]
        d_end = b_last - bcol + icol
        m_new = jnp.maximum(b_last + m_prev, jnp.max(d_end, axis=0, keepdims=True))
        wk = jnp.exp(d_end - m_new) * k
        scale = jnp.exp(b_last + m_prev - m_new)
        c_ref[h] = scale * cm + _dot_tn(wk, v)
        n_ref[h:h + 1, :] = scale * nv + jnp.sum(wk, axis=0, keepdims=True)
        m_ref[h:h + 1, :] = jnp.broadcast_to(m_new, (1, m_ref.shape[1]))

        hid = hid * lax.rsqrt(jnp.mean(hid * hid, axis=-1, keepdims=True) + RMS_EPS) * normw_ref[...]
        o_ref[0, :, h * ML_DV:(h + 1) * ML_DV] = (hid * _sigmoid(o_pre)).astype(o_ref.dtype)

    @pl.when(step == pl.num_programs(1) - 1)
    def _():
        co_ref[0] = c_ref[...]
        no_ref[0] = n_ref[...]
        mo_ref[0] = m_ref[:, 0:1]


def mlstm_group(proj, gates, gates_t, c0, n0, m0, in_slot, gate_b, norm_w, *, row0, nb, t, c,
                slot, n_slots, prev, tables=None, table_layer=0):
    tokens = proj.shape[0]
    nchunk = t // c
    blk0 = row0 // c
    nh = ML_HEADS
    proj3 = proj.reshape(tokens // c, c, proj.shape[1])
    ig = gates[:, :nh].reshape(tokens // c, c, nh)
    fg = gates[:, nh:].reshape(tokens // c, c, nh)
    ig_t = gates_t[:nh].reshape(nh, tokens // c, c).transpose(1, 0, 2)
    fg_t = gates_t[nh:].reshape(nh, tokens // c, c).transpose(1, 0, 2)

    def rows(b, s):
        return blk0 + b * nchunk + s

    kernel = functools.partial(_mlstm_kernel, c=c, cast=tables is not None)
    cast_args, cast_in, cast_out, cast_shape = _cast_specs(tables, table_layer, nb * nchunk,
                                                           lambda b, s: b * nchunk + s)
    prev_args, prev_specs, aliases = _alias_args(prev, first_input=13 + len(cast_args), first_output=1)
    o, cm, nv, m, *casts = pl.pallas_call(
        kernel,
        grid=(nb, nchunk),
        input_output_aliases=aliases,
        in_specs=[
            pl.BlockSpec((1, c, ML_MAIN_DIM), lambda b, s: (rows(b, s), 0, 0)),
            pl.BlockSpec((1, c, nh), lambda b, s: (rows(b, s), 0, 0)),
            pl.BlockSpec((1, c, nh), lambda b, s: (rows(b, s), 0, 0)),
            pl.BlockSpec((1, nh, c), lambda b, s: (rows(b, s), 0, 0)),
            pl.BlockSpec((1, nh, c), lambda b, s: (rows(b, s), 0, 0)),
            pl.BlockSpec((None, 1, nh, ML_DQK, ML_DV), lambda b, s: (in_slot, b, 0, 0, 0)),
            pl.BlockSpec((None, 1, nh, ML_DQK), lambda b, s: (in_slot, b, 0, 0)),
            pl.BlockSpec((None, 1, nh, 1), lambda b, s: (in_slot, b, 0, 0)),
            pl.BlockSpec((1, nh), lambda b, s: (0, 0)),
            pl.BlockSpec((1, nh), lambda b, s: (0, 0)),
            pl.BlockSpec((nh, 1), lambda b, s: (0, 0)),
            pl.BlockSpec((nh, 1), lambda b, s: (0, 0)),
            pl.BlockSpec((1, ML_DV), lambda b, s: (0, 0)),
        ] + cast_in + prev_specs,
        out_specs=[
            pl.BlockSpec((1, c, ML_V_DIM), lambda b, s: (b * nchunk + s, 0, 0)),
            pl.BlockSpec((None, 1, nh, ML_DQK, ML_DV), lambda b, s: (slot, b, 0, 0, 0)),
            pl.BlockSpec((None, 1, nh, ML_DQK), lambda b, s: (slot, b, 0, 0)),
            pl.BlockSpec((None, 1, nh, 1), lambda b, s: (slot, b, 0, 0)),
        ] + cast_out,
        out_shape=[
            jax.ShapeDtypeStruct((nb * nchunk, c, ML_V_DIM), BF16),
            jax.ShapeDtypeStruct((n_slots, nb, nh, ML_DQK, ML_DV), F32),
            jax.ShapeDtypeStruct((n_slots, nb, nh, ML_DQK), F32),
            jax.ShapeDtypeStruct((n_slots, nb, nh, 1), F32),
        ] + cast_shape,
        scratch_shapes=[pltpu.VMEM((nh, ML_DQK, ML_DV), F32),
                        pltpu.VMEM((nh, ML_DQK), F32),
                        pltpu.VMEM((nh, 128), F32),
                        pltpu.VMEM((nh, c, c), F32),
                        pltpu.VMEM((nh, c, ML_DV), F32)],
        compiler_params=_cparams("parallel", "arbitrary"),
        name=f"mlstm_chunk{c}",
    )(proj3, ig, fg, ig_t, fg_t, c0, n0, m0.reshape(m0.shape + (1,)),
      gate_b[:nh].reshape(1, nh), gate_b[nh:].reshape(1, nh),
      gate_b[:nh].reshape(nh, 1), gate_b[nh:].reshape(nh, 1), norm_w.reshape(1, ML_DV), *cast_args, *prev_args)
    return o.reshape(nb * t, ML_V_DIM), (cm, nv, m), casts


_CAND_PAIRS = [(a, b) for a in range(PEER_TOPK) for b in range(PEER_TOPK) if (a + 1) * (b + 1) <= PEER_TOPK]
_CAND_ROWS = -(-len(_CAND_PAIRS) // 8) * 8


def _top_values(work, count):
    rows = lax.broadcasted_iota(jnp.int32, work.shape, 0)
    out = []
    for r in range(count):
        m = jnp.max(work, axis=0, keepdims=True)
        out.append(m)
        if r + 1 < count:
            first = jnp.min(jnp.where(work == m, rows, work.shape[0]), axis=0, keepdims=True)
            work = jnp.where(rows == first, NEG_INF, work)
    return out


def _sorting_network(n):
    pairs = []
    p = 1
    while p < n:
        k = p
        while k >= 1:
            for j in range(k % p, n - k, 2 * k):
                for i in range(min(k, n - j - k)):
                    if (i + j) // (2 * p) == (i + j + k) // (2 * p):
                        pairs.append((i + j, i + j + k))
            k //= 2
        p *= 2
    return pairs


def _compare_exchange(vals, i, j):
    vals[i], vals[j] = jnp.maximum(vals[i], vals[j]), jnp.minimum(vals[i], vals[j])


def _top16_of_128(s):
    k = PEER_TOPK
    slab = s.shape[0] // k
    vals = [s[i * slab:(i + 1) * slab, :] for i in range(k)]
    for i, j in _sorting_network(k):
        _compare_exchange(vals, i, j)
    shift = slab // 2
    while shift >= 1:
        vals = [jnp.maximum(vals[i], pltpu.roll(vals[k - 1 - i], shift, 0)) for i in range(k)]
        stride = k // 2
        while stride >= 1:
            for i in range(k):
                if (i // stride) % 2 == 0:
                    _compare_exchange(vals, i, i + stride)
            stride //= 2
        shift //= 2
    return [v[0:1, :] for v in vals]


def _peer_route_kernel(q_ref, keys_ref, th1_ref, w1_ref, w2_ref, cand_ref):
    cand_ref[...] = jnp.full(cand_ref.shape, NEG_INF, F32)
    for h in range(PEER_HEADS):
        scores, tops = [], []
        for p in range(2):
            col = (2 * h + p) * PEER_HALF
            s = _dot_nt(keys_ref[h, p], q_ref[:, col:col + PEER_HALF])
            scores.append(s)
            tops.append(_top16_of_128(s))
        for i, (a, b) in enumerate(_CAND_PAIRS):
            cand_ref[i:i + 1, :] = tops[0][a] + tops[1][b]
        cand = cand_ref[...]
        tau = _top_values(cand, PEER_TOPK)[-1]
        max1, max2 = tops[0][0], tops[1][0]
        z = jnp.sum(jnp.where(cand >= tau, jnp.exp(cand - (max1 + max2)), 0.0), axis=0, keepdims=True)
        e2 = jnp.exp(scores[1] - max2)
        e2_tops = _top16_of_128(e2)
        cut = jnp.full(scores[0].shape, POS_INF, F32)
        for a in range(PEER_TOPK):
            cut_a = jnp.full(tau.shape, POS_INF, F32)
            for b in range(PEER_TOPK // (a + 1)):
                cut_a = jnp.where(tops[0][a] + tops[1][b] >= tau, e2_tops[b], cut_a)
            cut = jnp.where(scores[0] == tops[0][a], cut_a, cut)
        th1_ref[h] = cut
        w1_ref[h] = jnp.exp(scores[0] - max1) / z
        w2_ref[h] = e2


def peer_route(q, keys, *, tm):
    m = q.shape[0]
    nh = PEER_HEADS
    table = jax.ShapeDtypeStruct((nh, PEER_N_KEYS, m), F32)
    table_spec = pl.BlockSpec((nh, PEER_N_KEYS, tm), lambda i: (0, 0, i))
    return pl.pallas_call(
        _peer_route_kernel,
        grid=(m // tm,),
        in_specs=[pl.BlockSpec((tm, q.shape[1]), lambda i: (i, 0)),
                  pl.BlockSpec(keys.shape, lambda i: (0, 0, 0, 0))],
        out_specs=[table_spec] * 3,
        out_shape=[table] * 3,
        scratch_shapes=[pltpu.VMEM((_CAND_ROWS, tm), F32)],
        compiler_params=_cparams("parallel"),
        name="peer_route",
    )(q, keys)


def _peer_main_kernel(xb_ref, u_ref, vt_ref, th1_ref, w1_ref, w2_ref, x_ref, g_ref, b_ref,
                      o_ref, ob_ref, acc_ref, ht_ref, a_ref, *, te):
    e = pl.program_id(1)
    groups = te // PEER_N_KEYS
    tm = ht_ref.shape[1]

    @pl.when(e == 0)
    def _():
        acc_ref[...] = jnp.zeros_like(acc_ref)

    ht_ref[...] = _dot_nt(u_ref[...], xb_ref[...])
    for cc in range(groups):
        key1 = e * groups + cc
        rows = slice(cc * PEER_N_KEYS, (cc + 1) * PEER_N_KEYS)
        for t0 in range(0, tm, 2 * LANE_TILE):
            pair = slice(t0, t0 + 2 * LANE_TILE)
            th_rows = [th1_ref[h, pl.ds(key1, 1), pair] for h in range(PEER_HEADS)]
            w1_rows = [w1_ref[h, pl.ds(key1, 1), pair] for h in range(PEER_HEADS)]
            for half in range(2):
                lanes = slice(t0 + half * LANE_TILE, t0 + (half + 1) * LANE_TILE)
                sub = slice(half * LANE_TILE, (half + 1) * LANE_TILE)
                gate = jnp.zeros((PEER_N_KEYS, LANE_TILE), F32)
                for h in range(PEER_HEADS):
                    w2 = w2_ref[h, :, lanes]
                    gate = gate + jnp.where(w2 >= th_rows[h][:, sub], w2, 0.0) * w1_rows[h][:, sub]
                a_ref[rows, lanes] = (_gelu_tanh(ht_ref[rows, lanes]) * gate).astype(BF16)
    acc_ref[...] += _dot(vt_ref[...], a_ref[...])

    @pl.when(e == pl.num_programs(1) - 1)
    def _():
        for t0 in range(0, tm, LANE_TILE):
            tok = slice(t0, t0 + LANE_TILE)
            y = _layer_norm_rows(DEEPNORM_ALPHA * x_ref[tok, :] + acc_ref[:, tok].T, g_ref[...], b_ref[...])
            o_ref[tok, :] = y
            ob_ref[tok, :] = y.astype(BF16)


def peer_main(xb, x, u, vt, th1, w1, w2, g, b, *, tm, te):
    m, d = x.shape
    ne = u.shape[0]
    nh = PEER_HEADS
    kernel = functools.partial(_peer_main_kernel, te=te)
    once = pl.Buffered(1)
    table_spec = pl.BlockSpec((nh, PEER_N_KEYS, tm), lambda i, e: (0, 0, i), pipeline_mode=once)
    return pl.pallas_call(
        kernel,
        grid=(m // tm, ne // te),
        in_specs=[pl.BlockSpec((tm, d), lambda i, e: (i, 0), pipeline_mode=once),
                  pl.BlockSpec((te, d), lambda i, e: (e, 0)),
                  pl.BlockSpec((d, te), lambda i, e: (0, e)),
                  table_spec, table_spec, table_spec,
                  pl.BlockSpec((tm, d), lambda i, e: (i, 0), pipeline_mode=once),
                  pl.BlockSpec((1, d), lambda i, e: (0, 0)),
                  pl.BlockSpec((1, d), lambda i, e: (0, 0))],
        out_specs=[pl.BlockSpec((tm, d), lambda i, e: (i, 0), pipeline_mode=once),
                   pl.BlockSpec((tm, d), lambda i, e: (i, 0), pipeline_mode=once)],
        out_shape=[jax.ShapeDtypeStruct((m, d), F32), jax.ShapeDtypeStruct((m, d), BF16)],
        scratch_shapes=[pltpu.VMEM((d, tm), F32), pltpu.VMEM((te, tm), F32), pltpu.VMEM((te, tm), BF16)],
        compiler_params=_cparams("parallel", "arbitrary"),
        name="peer_main",
    )(xb, u, vt, th1, w1, w2, x, g.reshape(1, d), b.reshape(1, d))


TOKEN_TILE = 3072
GATE_TOKEN_TILE = 1024
PROJ_COL_TILE = 512
OUT_K_TILE = 512
OUT_ROW_TILE = 1024
PEER_TOKEN_TILE = 512
PEER_EXPERT_TILE = 1024


def kernel(x_prompt, x_sample, state_gdn_conv, state_gdn_s, state_mlstm_c, state_mlstm_n, state_mlstm_m,
           gdn_w_in, gdn_conv_w, gdn_a_log, gdn_dt_bias, gdn_norm_w, gdn_w_out,
           ml_w_in, ml_gate_b, ml_norm_w, ml_w_out,
           ln_mix_g, ln_mix_b, ln_ffn_g, ln_ffn_b,
           peer_w_q, peer_keys, peer_u, peer_v):
    bp, tp, d = x_prompt.shape
    bs, ts, _ = x_sample.shape
    np_tok = bp * tp
    ns_tok = bs * ts
    x = jnp.concatenate([x_prompt.reshape(np_tok, d), x_sample.reshape(ns_tok, d)], axis=0)
    xb = x.astype(BF16)
    cp = min(GDN_CHUNK, tp)
    cs = min(GDN_CHUNK, ts)

    n_gdn = state_gdn_s.shape[0]
    n_ml = state_mlstm_c.shape[0]
    zero_gdn = (jnp.zeros((1, bp) + state_gdn_conv.shape[2:], F32), jnp.zeros((1, bp) + state_gdn_s.shape[2:], F32))
    zero_ml = (jnp.zeros((1, bp) + state_mlstm_c.shape[2:], F32), jnp.zeros((1, bp) + state_mlstm_n.shape[2:], F32),
               jnp.zeros((1, bp) + state_mlstm_m.shape[2:], F32))
    p_gdn = s_gdn = p_ml = s_ml = None
    for layer in range(DEPTH):
        j = layer // 2
        prompt = dict(row0=0, nb=bp, t=tp, c=cp, slot=j)
        sample = dict(row0=np_tok, nb=bs, t=ts, c=cs, slot=j)
        if layer % 2 == 0:
            proj = matmul(xb, gdn_w_in, j, GDN_MAIN_DIM, tm=TOKEN_TILE, tn=PROJ_COL_TILE)
            gates, gates_t = gate_proj(xb, gdn_w_in[j, :, GDN_MAIN_DIM:], tm=GATE_TOKEN_TILE)
            shared = (gdn_conv_w[j], gdn_a_log[j], gdn_dt_bias[j], gdn_norm_w[j])
            o_p, p_gdn, (u_bf16, vt_bf16) = gdn_group(proj, gates, gates_t, *zero_gdn, 0, *shared, n_slots=n_gdn,
                                                      prev=p_gdn, tables=(peer_u, peer_v), table_layer=layer, **prompt)
            o_s, s_gdn, _ = gdn_group(proj, gates, gates_t, state_gdn_conv, state_gdn_s, j, *shared,
                                      n_slots=n_gdn, prev=s_gdn, **sample)
            w_out = gdn_w_out
        else:
            proj = matmul(xb, ml_w_in, j, ML_MAIN_DIM, tm=TOKEN_TILE, tn=PROJ_COL_TILE)
            gates, gates_t = gate_proj(xb, ml_w_in[j, :, ML_MAIN_DIM:], tm=GATE_TOKEN_TILE)
            shared = (ml_gate_b[j], ml_norm_w[j])
            o_p, p_ml, (u_bf16, vt_bf16) = mlstm_group(proj, gates, gates_t, *zero_ml, 0, *shared, n_slots=n_ml,
                                                       prev=p_ml, tables=(peer_u, peer_v), table_layer=layer, **prompt)
            o_s, s_ml, _ = mlstm_group(proj, gates, gates_t, state_mlstm_c, state_mlstm_n, state_mlstm_m, j, *shared,
                                       n_slots=n_ml, prev=s_ml, **sample)
            w_out = ml_w_out
        x, xb = out_proj_ln(o_p, o_s, w_out, j, x, ln_mix_g[layer], ln_mix_b[layer],
                            tm=OUT_ROW_TILE, tk=OUT_K_TILE)

        q = matmul(xb, peer_w_q, layer, peer_w_q.shape[2], tm=TOKEN_TILE, tn=PROJ_COL_TILE)
        tables = peer_route(q, peer_keys[layer], tm=PEER_TOKEN_TILE)
        x, xb = peer_main(xb, x, u_bf16, vt_bf16, *tables,
                          ln_ffn_g[layer], ln_ffn_b[layer], tm=PEER_TOKEN_TILE, te=PEER_EXPERT_TILE)

    y_prompt = x[:np_tok].reshape(bp, tp, d)
    y_sample = x[np_tok:].reshape(bs, ts, d)
    return (y_prompt, y_sample,
            p_gdn[0], p_gdn[1], p_ml[0], p_ml[1], p_ml[2].reshape(n_ml, bp, ML_HEADS),
            s_gdn[0], s_gdn[1], s_ml[0], s_ml[1], s_ml[2].reshape(n_ml, bs, ML_HEADS))
```

```python
import functools

import jax
import jax.numpy as jnp
from jax import lax
from jax.experimental import pallas as pl
from jax.experimental.pallas import tpu as pltpu

F32 = jnp.float32
BF16 = jnp.bfloat16

D_MODEL = 2048
DEPTH = 4
GDN_K_HEADS = 16
GDN_V_HEADS = 32
GDN_DK = 128
GDN_DV = 128
GDN_QK_DIM = GDN_K_HEADS * GDN_DK
GDN_V_DIM = GDN_V_HEADS * GDN_DV
GDN_CONV_DIM = 2 * GDN_QK_DIM + GDN_V_DIM
GDN_MAIN_DIM = GDN_CONV_DIM + GDN_V_DIM
GDN_CONV_W = 4
GDN_CHUNK = 64
ML_HEADS = 8
ML_DQK = 128
ML_DV = 256
ML_QK_DIM = ML_HEADS * ML_DQK
ML_V_DIM = ML_HEADS * ML_DV
ML_MAIN_DIM = 2 * ML_QK_DIM + 2 * ML_V_DIM
ML_CHUNK = 64
PEER_HEADS = 8
PEER_N_KEYS = 128
PEER_HALF = 128
PEER_TOPK = 16
DEEPNORM_ALPHA = (2 * DEPTH) ** 0.25
LN_EPS = 1e-5
RMS_EPS = 1e-6
NEG_BIG = -1e30
NEG_INF = float("-inf")
POS_INF = float("inf")

VMEM_LIMIT_BYTES = 56 * 1024 * 1024
LANE_TILE = 128
HIGHEST = lax.Precision.HIGHEST


def _cparams(*sem):
    return pltpu.CompilerParams(dimension_semantics=sem, vmem_limit_bytes=VMEM_LIMIT_BYTES)


def _dot(a, b):
    return jnp.dot(a.astype(BF16), b.astype(BF16), preferred_element_type=F32)


def _dot_nt(a, b):
    return lax.dot_general(a.astype(BF16), b.astype(BF16), (((1,), (1,)), ((), ())),
                           preferred_element_type=F32)


def _dot_tn(a, b):
    return lax.dot_general(a.astype(BF16), b.astype(BF16), (((0,), (0,)), ((), ())),
                           preferred_element_type=F32)


def _dot_f32(a, b):
    return jnp.dot(a, b, preferred_element_type=F32, precision=HIGHEST)


def _sigmoid(x):
    return 1.0 / (1.0 + jnp.exp(-x))


def _silu(x):
    return x * _sigmoid(x)


def _softplus(x):
    return jnp.maximum(x, 0.0) + jnp.log(1.0 + jnp.exp(-jnp.abs(x)))


def _log_sigmoid(x):
    return -_softplus(-x)


def _gelu_tanh(x):
    return 0.5 * x * (1.0 + jnp.tanh(0.7978845608028654 * (x + 0.044715 * (x * x * x))))


def _lower(c, strict):
    r = lax.broadcasted_iota(jnp.int32, (c, c), 0)
    k = lax.broadcasted_iota(jnp.int32, (c, c), 1)
    return (r > k) if strict else (r >= k)


def _upper(c):
    r = lax.broadcasted_iota(jnp.int32, (c, c), 0)
    k = lax.broadcasted_iota(jnp.int32, (c, c), 1)
    return r <= k


def _layer_norm_rows(v, g, b):
    mu = jnp.mean(v, axis=-1, keepdims=True)
    d = v - mu
    var = jnp.mean(d * d, axis=-1, keepdims=True)
    return d * lax.rsqrt(var + LN_EPS) * g + b


def _matmul_kernel(x_ref, w_ref, o_ref):
    o_ref[...] = _dot(x_ref[...], w_ref[...])


def matmul(x, w_stack, layer, n_cols, *, tm, tn):
    m, k = x.shape
    return pl.pallas_call(
        _matmul_kernel,
        grid=(m // tm, n_cols // tn),
        in_specs=[pl.BlockSpec((tm, k), lambda i, j: (i, 0), pipeline_mode=pl.Buffered(1)),
                  pl.BlockSpec((None, k, tn), lambda i, j: (layer, 0, j))],
        out_specs=pl.BlockSpec((tm, tn), lambda i, j: (i, j)),
        out_shape=jax.ShapeDtypeStruct((m, n_cols), F32),
        compiler_params=_cparams("parallel", "arbitrary"),
        name="proj_matmul",
    )(x, w_stack)


def _gate_proj_kernel(x_ref, w_ref, wt_ref, o_ref, ot_ref):
    x = x_ref[...]
    o_ref[...] = _dot(x, w_ref[...])
    ot_ref[...] = _dot_nt(wt_ref[...], x)


def gate_proj(x, w_gate, *, tm):
    m, k = x.shape
    n = w_gate.shape[1]
    return pl.pallas_call(
        _gate_proj_kernel,
        grid=(m // tm,),
        in_specs=[pl.BlockSpec((tm, k), lambda i: (i, 0)),
                  pl.BlockSpec((k, n), lambda i: (0, 0)),
                  pl.BlockSpec((n, k), lambda i: (0, 0))],
        out_specs=[pl.BlockSpec((tm, n), lambda i: (i, 0)),
                   pl.BlockSpec((n, tm), lambda i: (0, i))],
        out_shape=[jax.ShapeDtypeStruct((m, n), F32), jax.ShapeDtypeStruct((n, m), F32)],
        compiler_params=_cparams("parallel"),
        name="gate_proj",
    )(x, w_gate, w_gate.T)


def _out_proj_ln_kernel(hp_ref, hs_ref, w_ref, x_ref, g_ref, b_ref, o_ref, ob_ref, acc_ref, *, prompt_tiles):
    i = pl.program_id(0)
    kk = pl.program_id(1)

    @pl.when(kk == 0)
    def _():
        acc_ref[...] = jnp.zeros_like(acc_ref)

    @pl.when(i < prompt_tiles)
    def _():
        acc_ref[...] += _dot(hp_ref[...], w_ref[...])

    @pl.when(i >= prompt_tiles)
    def _():
        acc_ref[...] += _dot(hs_ref[...], w_ref[...])

    @pl.when(kk == pl.num_programs(1) - 1)
    def _():
        y = _layer_norm_rows(DEEPNORM_ALPHA * x_ref[...] + acc_ref[...], g_ref[...], b_ref[...])
        o_ref[...] = y
        ob_ref[...] = y.astype(BF16)


def out_proj_ln(h_prompt, h_sample, w_stack, layer, x, g, b, *, tm, tk):
    mp, k = h_prompt.shape
    ms = h_sample.shape[0]
    d = w_stack.shape[2]
    prompt_tiles = mp // tm
    last_k = k // tk - 1
    kernel = functools.partial(_out_proj_ln_kernel, prompt_tiles=prompt_tiles)
    return pl.pallas_call(
        kernel,
        grid=((mp + ms) // tm, k // tk),
        in_specs=[pl.BlockSpec((tm, tk), lambda i, j: (jnp.minimum(i, prompt_tiles - 1),
                                                        jnp.where(i < prompt_tiles, j, last_k))),
                  pl.BlockSpec((tm, tk), lambda i, j: (jnp.maximum(i - prompt_tiles, 0),
                                                        jnp.where(i < prompt_tiles, 0, j))),
                  pl.BlockSpec((None, tk, d), lambda i, j: (layer, j, 0)),
                  pl.BlockSpec((tm, d), lambda i, j: (i, 0), pipeline_mode=pl.Buffered(1)),
                  pl.BlockSpec((1, d), lambda i, j: (0, 0)),
                  pl.BlockSpec((1, d), lambda i, j: (0, 0))],
        out_specs=[pl.BlockSpec((tm, d), lambda i, j: (i, 0), pipeline_mode=pl.Buffered(1)),
                   pl.BlockSpec((tm, d), lambda i, j: (i, 0), pipeline_mode=pl.Buffered(1))],
        out_shape=[jax.ShapeDtypeStruct((mp + ms, d), F32), jax.ShapeDtypeStruct((mp + ms, d), BF16)],
        scratch_shapes=[pltpu.VMEM((tm, d), F32)],
        compiler_params=_cparams("parallel", "arbitrary"),
        name="out_proj_ln",
    )(h_prompt, h_sample, w_stack, x, g.reshape(1, d), b.reshape(1, d))


CONV_PAD = 8


def _dot_split(e, r):
    c = e.shape[0]
    e_hi = e.astype(BF16).astype(F32)
    r_hi = r.astype(BF16).astype(F32)
    stacked = _dot(jnp.concatenate([e_hi, e - e_hi], axis=0), r_hi)
    return stacked[0:c] + stacked[c:2 * c] + _dot(e_hi, r - r_hi)


def _cast_table_slab(u_ref, v_ref, ub_ref, vt_ref):
    ub_ref[...] = u_ref[...].astype(BF16)
    vt_ref[...] = v_ref[...].T.astype(BF16)


def _cast_specs(tables, layer, steps, step_index):
    if tables is None:
        return [], [], [], []
    ne, d = tables[0].shape[1:]
    slab = ne // steps
    in_specs = [pl.BlockSpec((None, slab, d), lambda b, s: (layer, step_index(b, s), 0))] * 2
    out_specs = [pl.BlockSpec((slab, d), lambda b, s: (step_index(b, s), 0)),
                 pl.BlockSpec((d, slab), lambda b, s: (0, step_index(b, s)))]
    out_shape = [jax.ShapeDtypeStruct((ne, d), BF16), jax.ShapeDtypeStruct((d, ne), BF16)]
    return list(tables), in_specs, out_specs, out_shape


def _gdn_kernel(qkv_ref, z_ref, bpre_ref, apre_ref, bpre_t_ref, apre_t_ref, conv0_ref, s0_ref,
                convw_ref, alog_ref, dtb_ref, alog_t_ref, dtb_t_ref, normw_ref, *rest, c, cast):
    (xp_ref, s_ref, kq_ref, l_ref, a_ref, e_ref, p_ref, rhs_ref, lhs_ref, kd_ref, vn_ref) = rest[-11:]
    if cast:
        _cast_table_slab(rest[0], rest[1], rest[-13], rest[-12])
        o_ref, convo_ref, so_ref = rest[-16:-13]
    else:
        o_ref, convo_ref, so_ref = rest[-14:-11]
    step = pl.program_id(1)
    hist = GDN_CONV_W - 1
    heads = range(GDN_V_HEADS)
    rep = GDN_V_HEADS // GDN_K_HEADS
    tok = slice(CONV_PAD, CONV_PAD + c)

    @pl.when(step == 0)
    def _():
        xp_ref[CONV_PAD - hist:CONV_PAD, :] = conv0_ref[0]
        s_ref[...] = s0_ref[0]

    x = qkv_ref[0]
    xp_ref[tok, :] = x
    y = xp_ref[CONV_PAD - hist:CONV_PAD - hist + c, :] * convw_ref[0:1, :]
    for i in range(1, GDN_CONV_W):
        y = y + xp_ref[CONV_PAD - hist + i:CONV_PAD - hist + i + c, :] * convw_ref[i:i + 1, :]
    tail = x[c - hist:c, :]
    xp_ref[CONV_PAD - hist:CONV_PAD, :] = tail
    convo_ref[0] = tail
    xp_ref[tok, :] = _silu(y)

    beta_c = _sigmoid(bpre_ref[0])
    g_c = -jnp.exp(alog_ref[...]) * _softplus(apre_ref[0] + dtb_ref[...])
    g_r = -jnp.exp(alog_t_ref[...]) * _softplus(apre_t_ref[0] + dtb_t_ref[...])
    rows = lax.broadcasted_iota(jnp.int32, (c, c), 0)
    cols = lax.broadcasted_iota(jnp.int32, (c, c), 1)
    causal = rows >= cols
    strict = rows > cols
    gc_c = _dot_f32(causal.astype(F32), g_c)
    gc_r = _dot_f32(g_r, (rows <= cols).astype(F32))

    for kh in range(GDN_K_HEADS):
        qs = slice(kh * GDN_DK, (kh + 1) * GDN_DK)
        ks = slice(GDN_QK_DIM + kh * GDN_DK, GDN_QK_DIM + (kh + 1) * GDN_DK)
        q = xp_ref[tok, qs]
        k = xp_ref[tok, ks]
        q = q * lax.rsqrt(jnp.sum(q * q, axis=-1, keepdims=True) + RMS_EPS) * (GDN_DK ** -0.5)
        k = k * lax.rsqrt(jnp.sum(k * k, axis=-1, keepdims=True) + RMS_EPS)
        xp_ref[tok, qs] = q
        xp_ref[tok, ks] = k
        kq_ref[kh] = _dot_nt(jnp.concatenate([k, q], axis=0), k)

    diag8 = strict & ((rows >> 3) == (cols >> 3))
    for h in heads:
        kh = h // rep
        q = xp_ref[tok, kh * GDN_DK:(kh + 1) * GDN_DK]
        k = xp_ref[tok, GDN_QK_DIM + kh * GDN_DK:GDN_QK_DIM + (kh + 1) * GDN_DK]
        v = xp_ref[tok, 2 * GDN_QK_DIM + h * GDN_DV:2 * GDN_QK_DIM + (h + 1) * GDN_DV]
        gcol = gc_c[:, h:h + 1]
        bcol = beta_c[:, h:h + 1]
        diff = gcol - gc_r[h:h + 1, :]
        egc = jnp.exp(gcol)
        l = bcol * kq_ref[kh, 0:c, :] * jnp.exp(jnp.where(strict, diff, NEG_BIG))
        l_ref[h] = l
        a_ref[h] = kq_ref[kh, c:2 * c, :] * jnp.exp(jnp.where(causal, diff, NEG_BIG))
        rhs_ref[h, :, 0:GDN_DV] = v * bcol
        rhs_ref[h, :, GDN_DV:GDN_DV + GDN_DK] = k * (bcol * egc)
        lhs_ref[h, c:2 * c, :] = q * egc
        kd_ref[h] = k * jnp.exp(gcol[c - 1:c, :] - gcol)
        l8 = jnp.where(diag8, l, 0.0)
        e_ref[h] = -l8
        p_ref[h] = _dot(l8, l8)
    for h in heads:
        e = e_ref[h]
        p = p_ref[h]
        e_ref[h] = e + p + _dot(e, p)
        p_ref[h] = _dot(p, p)
    for h in heads:
        e = e_ref[h]
        p = p_ref[h]
        e_ref[h] = e + p + _dot(e, p)
    shift = 3
    while (2 << shift) <= c:
        lower_left = (((rows >> (shift + 1)) == (cols >> (shift + 1)))
                      & (((rows >> shift) & 1) == 1) & (((cols >> shift) & 1) == 0))
        for h in heads:
            n = jnp.where(lower_left, l_ref[h], 0.0)
            p_ref[h] = n + _dot(e_ref[h], n)
        for h in heads:
            t = p_ref[h]
            e = e_ref[h]
            e_ref[h] = e - t - _dot(t, e)
        shift += 1

    for h in heads:
        r = rhs_ref[h]
        uw = r + _dot_split(e_ref[h], r)
        rhs_ref[h, :, 0:GDN_DV] = uw[:, 0:GDN_DV]
        lhs_ref[h, 0:c, :] = uw[:, GDN_DV:GDN_DV + GDN_DK]
    for h in heads:
        ws = _dot(lhs_ref[h], s_ref[h])
        vn_ref[h] = rhs_ref[h, :, 0:GDN_DV] - ws[0:c]
        lhs_ref[h, c:2 * c, :] = ws[c:2 * c]
    for h in heads:
        v_new = vn_ref[h]
        o = lhs_ref[h, c:2 * c, :] + _dot(a_ref[h], v_new)
        s_ref[h] = s_ref[h] * jnp.exp(gc_c[c - 1:c, h:h + 1]) + _dot_tn(kd_ref[h], v_new)
        o = o * lax.rsqrt(jnp.mean(o * o, axis=-1, keepdims=True) + RMS_EPS) * normw_ref[...]
        zh = z_ref[0, :, h * GDN_DV:(h + 1) * GDN_DV]
        o_ref[0, :, h * GDN_DV:(h + 1) * GDN_DV] = (o * _silu(zh)).astype(o_ref.dtype)

    @pl.when(step == pl.num_programs(1) - 1)
    def _():
        so_ref[0] = s_ref[...]


def _alias_args(prev, first_input, first_output):
    if prev is None:
        return [], [], {}
    specs = [pl.BlockSpec(memory_space=pl.ANY)] * len(prev)
    aliases = {first_input + i: first_output + i for i in range(len(prev))}
    return list(prev), specs, aliases


def gdn_group(proj, gates, gates_t, conv0, s0, in_slot, conv_w, a_log, dt_bias, norm_w, *, row0, nb, t, c,
              slot, n_slots, prev, tables=None, table_layer=0):
    tokens = proj.shape[0]
    nchunk = t // c
    blk0 = row0 // c
    hv = GDN_V_HEADS
    proj3 = proj.reshape(tokens // c, c, proj.shape[1])
    bpre = gates[:, :hv].reshape(tokens // c, c, hv)
    apre = gates[:, hv:].reshape(tokens // c, c, hv)
    bpre_t = gates_t[:hv].reshape(hv, tokens // c, c).transpose(1, 0, 2)
    apre_t = gates_t[hv:].reshape(hv, tokens // c, c).transpose(1, 0, 2)
    z_blk = GDN_CONV_DIM // GDN_V_DIM

    def rows(b, s):
        return blk0 + b * nchunk + s

    kernel = functools.partial(_gdn_kernel, c=c, cast=tables is not None)
    cast_args, cast_in, cast_out, cast_shape = _cast_specs(tables, table_layer, nb * nchunk,
                                                           lambda b, s: b * nchunk + s)
    prev_args, prev_specs, aliases = _alias_args(prev, first_input=14 + len(cast_args), first_output=1)
    o, conv, s, *casts = pl.pallas_call(
        kernel,
        grid=(nb, nchunk),
        input_output_aliases=aliases,
        in_specs=[
            pl.BlockSpec((1, c, GDN_CONV_DIM), lambda b, s: (rows(b, s), 0, 0)),
            pl.BlockSpec((1, c, GDN_V_DIM), lambda b, s: (rows(b, s), 0, z_blk)),
            pl.BlockSpec((1, c, hv), lambda b, s: (rows(b, s), 0, 0)),
            pl.BlockSpec((1, c, hv), lambda b, s: (rows(b, s), 0, 0)),
            pl.BlockSpec((1, hv, c), lambda b, s: (rows(b, s), 0, 0)),
            pl.BlockSpec((1, hv, c), lambda b, s: (rows(b, s), 0, 0)),
            pl.BlockSpec((None, 1, GDN_CONV_W - 1, GDN_CONV_DIM), lambda b, s: (in_slot, b, 0, 0)),
            pl.BlockSpec((None, 1, hv, GDN_DK, GDN_DV), lambda b, s: (in_slot, b, 0, 0, 0)),
            pl.BlockSpec((GDN_CONV_W, GDN_CONV_DIM), lambda b, s: (0, 0)),
            pl.BlockSpec((1, hv), lambda b, s: (0, 0)),
            pl.BlockSpec((1, hv), lambda b, s: (0, 0)),
            pl.BlockSpec((hv, 1), lambda b, s: (0, 0)),
            pl.BlockSpec((hv, 1), lambda b, s: (0, 0)),
            pl.BlockSpec((1, GDN_DV), lambda b, s: (0, 0)),
        ] + cast_in + prev_specs,
        out_specs=[
            pl.BlockSpec((1, c, GDN_V_DIM), lambda b, s: (b * nchunk + s, 0, 0)),
            pl.BlockSpec((None, 1, GDN_CONV_W - 1, GDN_CONV_DIM), lambda b, s: (slot, b, 0, 0)),
            pl.BlockSpec((None, 1, hv, GDN_DK, GDN_DV), lambda b, s: (slot, b, 0, 0, 0)),
        ] + cast_out,
        out_shape=[
            jax.ShapeDtypeStruct((nb * nchunk, c, GDN_V_DIM), BF16),
            jax.ShapeDtypeStruct((n_slots, nb, GDN_CONV_W - 1, GDN_CONV_DIM), F32),
            jax.ShapeDtypeStruct((n_slots, nb, hv, GDN_DK, GDN_DV), F32),
        ] + cast_shape,
        scratch_shapes=[pltpu.VMEM((CONV_PAD + c, GDN_CONV_DIM), F32),
                        pltpu.VMEM((hv, GDN_DK, GDN_DV), F32),
                        pltpu.VMEM((GDN_K_HEADS, 2 * c, c), F32),
                        pltpu.VMEM((hv, c, c), F32),
                        pltpu.VMEM((hv, c, c), F32),
                        pltpu.VMEM((hv, c, c), F32),
                        pltpu.VMEM((hv, c, c), F32),
                        pltpu.VMEM((hv, c, GDN_DV + GDN_DK), F32),
                        pltpu.VMEM((hv, 2 * c, GDN_DK), F32),
                        pltpu.VMEM((hv, c, GDN_DK), F32),
                        pltpu.VMEM((hv, c, GDN_DV), F32)],
        compiler_params=_cparams("parallel", "arbitrary"),
        name=f"gdn_chunk{c}",
    )(proj3, proj3, bpre, apre, bpre_t, apre_t, conv0, s0, conv_w,
      a_log.reshape(1, hv), dt_bias.reshape(1, hv), a_log.reshape(hv, 1), dt_bias.reshape(hv, 1),
      norm_w.reshape(1, GDN_DV), *cast_args, *prev_args)
    return o.reshape(nb * t, GDN_V_DIM), (conv, s), casts


def _mlstm_kernel(main_ref, ig_ref, fg_ref, ig_t_ref, fg_t_ref, c0_ref, n0_ref, m0_ref,
                  bi_ref, bf_ref, bi_t_ref, bf_t_ref, normw_ref, *rest, c, cast):
    c_ref, n_ref, m_ref, qk_ref, qc_ref = rest[-5:]
    if cast:
        _cast_table_slab(rest[0], rest[1], rest[-7], rest[-6])
        o_ref, co_ref, no_ref, mo_ref = rest[-11:-7]
    else:
        o_ref, co_ref, no_ref, mo_ref = rest[-9:-5]
    step = pl.program_id(1)

    @pl.when(step == 0)
    def _():
        c_ref[...] = c0_ref[0]
        n_ref[...] = n0_ref[0]
        m_ref[...] = jnp.broadcast_to(m0_ref[0], m_ref.shape)

    ig_c = ig_ref[0] + bi_ref[...]
    lf_c = _log_sigmoid(fg_ref[0] + bf_ref[...])
    ig_r = ig_t_ref[0] + bi_t_ref[...]
    lf_r = _log_sigmoid(fg_t_ref[0] + bf_t_ref[...])
    causal = _lower(c, strict=False)
    bc_c = _dot_f32(causal.astype(F32), lf_c)
    bc_r = _dot_f32(lf_r, _upper(c).astype(F32))

    for h in range(ML_HEADS):
        q = main_ref[0, :, h * ML_DQK:(h + 1) * ML_DQK]
        k = main_ref[0, :, ML_QK_DIM + h * ML_DQK:ML_QK_DIM + (h + 1) * ML_DQK] * (ML_DQK ** -0.5)
        qk_ref[h] = _dot_nt(q, k)
        qc_ref[h] = _dot(q, c_ref[h])

    for h in range(ML_HEADS):
        q = main_ref[0, :, h * ML_DQK:(h + 1) * ML_DQK]
        k = main_ref[0, :, ML_QK_DIM + h * ML_DQK:ML_QK_DIM + (h + 1) * ML_DQK] * (ML_DQK ** -0.5)
        v = main_ref[0, :, 2 * ML_QK_DIM + h * ML_DV:2 * ML_QK_DIM + (h + 1) * ML_DV]
        o_pre = main_ref[0, :, 2 * ML_QK_DIM + ML_V_DIM + h * ML_DV:2 * ML_QK_DIM + ML_V_DIM + (h + 1) * ML_DV]
        bcol = bc_c[:, h:h + 1]
        brow = bc_r[h:h + 1, :]
        icol = ig_c[:, h:h + 1]
        irow = ig_r[h:h + 1, :]
        m_prev = m_ref[h:h + 1, 0:1]
        cm = c_ref[h]
        nv = n_ref[h:h + 1, :]

        d = jnp.where(causal, bcol - brow + irow, NEG_BIG)
        inter = bcol + m_prev
        m_t = jnp.maximum(inter, jnp.max(d, axis=-1, keepdims=True))
        w_intra = jnp.exp(d - m_t)
        w_inter = jnp.exp(inter - m_t)
        sqk = qk_ref[h] * w_intra
        num = w_inter * qc_ref[h] + _dot(sqk, v)
        den = w_inter * jnp.sum(q * nv, axis=-1, keepdims=True) + jnp.sum(sqk, axis=-1, keepdims=True)
        hid = num / jnp.maximum(jnp.abs(den), jnp.exp(-m_t))

        b_last = bcol[c - 1:c, :]
        d_end = b_last - bcol + icol
        m_new = jnp.maximum(b_last + m_prev, jnp.max(d_end, axis=0, keepdims=True))
        wk = jnp.exp(d_end - m_new) * k
        scale = jnp.exp(b_last + m_prev - m_new)
        c_ref[h] = scale * cm + _dot_tn(wk, v)
        n_ref[h:h + 1, :] = scale * nv + jnp.sum(wk, axis=0, keepdims=True)
        m_ref[h:h + 1, :] = jnp.broadcast_to(m_new, (1, m_ref.shape[1]))

        hid = hid * lax.rsqrt(jnp.mean(hid * hid, axis=-1, keepdims=True) + RMS_EPS) * normw_ref[...]
        o_ref[0, :, h * ML_DV:(h + 1) * ML_DV] = (hid * _sigmoid(o_pre)).astype(o_ref.dtype)

    @pl.when(step == pl.num_programs(1) - 1)
    def _():
        co_ref[0] = c_ref[...]
        no_ref[0] = n_ref[...]
        mo_ref[0] = m_ref[:, 0:1]


def mlstm_group(proj, gates, gates_t, c0, n0, m0, in_slot, gate_b, norm_w, *, row0, nb, t, c,
                slot, n_slots, prev, tables=None, table_layer=0):
    tokens = proj.shape[0]
    nchunk = t // c
    blk0 = row0 // c
    nh = ML_HEADS
    proj3 = proj.reshape(tokens // c, c, proj.shape[1])
    ig = gates[:, :nh].reshape(tokens // c, c, nh)
    fg = gates[:, nh:].reshape(tokens // c, c, nh)
    ig_t = gates_t[:nh].reshape(nh, tokens // c, c).transpose(1, 0, 2)
    fg_t = gates_t[nh:].reshape(nh, tokens // c, c).transpose(1, 0, 2)

    def rows(b, s):
        return blk0 + b * nchunk + s

    kernel = functools.partial(_mlstm_kernel, c=c, cast=tables is not None)
    cast_args, cast_in, cast_out, cast_shape = _cast_specs(tables, table_layer, nb * nchunk,
                                                           lambda b, s: b * nchunk + s)
    prev_args, prev_specs, aliases = _alias_args(prev, first_input=13 + len(cast_args), first_output=1)
    o, cm, nv, m, *casts = pl.pallas_call(
        kernel,
        grid=(nb, nchunk),
        input_output_aliases=aliases,
        in_specs=[
            pl.BlockSpec((1, c, ML_MAIN_DIM), lambda b, s: (rows(b, s), 0, 0)),
            pl.BlockSpec((1, c, nh), lambda b, s: (rows(b, s), 0, 0)),
            pl.BlockSpec((1, c, nh), lambda b, s: (rows(b, s), 0, 0)),
            pl.BlockSpec((1, nh, c), lambda b, s: (rows(b, s), 0, 0)),
            pl.BlockSpec((1, nh, c), lambda b, s: (rows(b, s), 0, 0)),
            pl.BlockSpec((None, 1, nh, ML_DQK, ML_DV), lambda b, s: (in_slot, b, 0, 0, 0)),
            pl.BlockSpec((None, 1, nh, ML_DQK), lambda b, s: (in_slot, b, 0, 0)),
            pl.BlockSpec((None, 1, nh, 1), lambda b, s: (in_slot, b, 0, 0)),
            pl.BlockSpec((1, nh), lambda b, s: (0, 0)),
            pl.BlockSpec((1, nh), lambda b, s: (0, 0)),
            pl.BlockSpec((nh, 1), lambda b, s: (0, 0)),
            pl.BlockSpec((nh, 1), lambda b, s: (0, 0)),
            pl.BlockSpec((1, ML_DV), lambda b, s: (0, 0)),
        ] + cast_in + prev_specs,
        out_specs=[
            pl.BlockSpec((1, c, ML_V_DIM), lambda b, s: (b * nchunk + s, 0, 0)),
            pl.BlockSpec((None, 1, nh, ML_DQK, ML_DV), lambda b, s: (slot, b, 0, 0, 0)),
            pl.BlockSpec((None, 1, nh, ML_DQK), lambda b, s: (slot, b, 0, 0)),
            pl.BlockSpec((None, 1, nh, 1), lambda b, s: (slot, b, 0, 0)),
        ] + cast_out,
        out_shape=[
            jax.ShapeDtypeStruct((nb * nchunk, c, ML_V_DIM), BF16),
            jax.ShapeDtypeStruct((n_slots, nb, nh, ML_DQK, ML_DV), F32),
            jax.ShapeDtypeStruct((n_slots, nb, nh, ML_DQK), F32),
            jax.ShapeDtypeStruct((n_slots, nb, nh, 1), F32),
        ] + cast_shape,
        scratch_shapes=[pltpu.VMEM((nh, ML_DQK, ML_DV), F32),
                        pltpu.VMEM((nh, ML_DQK), F32),
                        pltpu.VMEM((nh, 128), F32),
                        pltpu.VMEM((nh, c, c), F32),
                        pltpu.VMEM((nh, c, ML_DV), F32)],
        compiler_params=_cparams("parallel", "arbitrary"),
        name=f"mlstm_chunk{c}",
    )(proj3, ig, fg, ig_t, fg_t, c0, n0, m0.reshape(m0.shape + (1,)),
      gate_b[:nh].reshape(1, nh), gate_b[nh:].reshape(1, nh),
      gate_b[:nh].reshape(nh, 1), gate_b[nh:].reshape(nh, 1), norm_w.reshape(1, ML_DV), *cast_args, *prev_args)
    return o.reshape(nb * t, ML_V_DIM), (cm, nv, m), casts


_CAND_PAIRS = [(a, b) for a in range(PEER_TOPK) for b in range(PEER_TOPK) if (a + 1) * (b + 1) <= PEER_TOPK]
_CAND_ROWS = -(-len(_CAND_PAIRS) // 8) * 8


def _top_values(work, count):
    rows = lax.broadcasted_iota(jnp.int32, work.shape, 0)
    out = []
    for r in range(count):
        m = jnp.max(work, axis=0, keepdims=True)
        out.append(m)
        if r + 1 < count:
            first = jnp.min(jnp.where(work == m, rows, work.shape[0]), axis=0, keepdims=True)
            work = jnp.where(rows == first, NEG_INF, work)
    return out


def _sorting_network(n):
    pairs = []
    p = 1
    while p < n:
        k = p
        while k >= 1:
            for j in range(k % p, n - k, 2 * k):
                for i in range(min(k, n - j - k)):
                    if (i + j) // (2 * p) == (i + j + k) // (2 * p):
                        pairs.append((i + j, i + j + k))
            k //= 2
        p *= 2
    return pairs


def _compare_exchange(vals, i, j):
    vals[i], vals[j] = jnp.maximum(vals[i], vals[j]), jnp.minimum(vals[i], vals[j])


def _top16_of_128(s):
    k = PEER_TOPK
    slab = s.shape[0] // k
    vals = [s[i * slab:(i + 1) * slab, :] for i in range(k)]
    for i, j in _sorting_network(k):
        _compare_exchange(vals, i, j)
    shift = slab // 2
    while shift >= 1:
        vals = [jnp.maximum(vals[i], pltpu.roll(vals[k - 1 - i], shift, 0)) for i in range(k)]
        stride = k // 2
        while stride >= 1:
            for i in range(k):
                if (i // stride) % 2 == 0:
                    _compare_exchange(vals, i, i + stride)
            stride //= 2
        shift //= 2
    return [v[0:1, :] for v in vals]


def _peer_route_kernel(q_ref, keys_ref, th1_ref, w1_ref, w2_ref, cand_ref):
    cand_ref[...] = jnp.full(cand_ref.shape, NEG_INF, F32)
    for h in range(PEER_HEADS):
        scores, tops = [], []
        for p in range(2):
            col = (2 * h + p) * PEER_HALF
            s = _dot_nt(keys_ref[h, p], q_ref[:, col:col + PEER_HALF])
            scores.append(s)
            tops.append(_top16_of_128(s))
        for i, (a, b) in enumerate(_CAND_PAIRS):
            cand_ref[i:i + 1, :] = tops[0][a] + tops[1][b]
        cand = cand_ref[...]
        tau = _top_values(cand, PEER_TOPK)[-1]
        max1, max2 = tops[0][0], tops[1][0]
        z = jnp.sum(jnp.where(cand >= tau, jnp.exp(cand - (max1 + max2)), 0.0), axis=0, keepdims=True)
        e2 = jnp.exp(scores[1] - max2)
        e2_tops = _top16_of_128(e2)
        cut = jnp.full(scores[0].shape, POS_INF, F32)
        for a in range(PEER_TOPK):
            cut_a = jnp.full(tau.shape, POS_INF, F32)
            for b in range(PEER_TOPK // (a + 1)):
                cut_a = jnp.where(tops[0][a] + tops[1][b] >= tau, e2_tops[b], cut_a)
            cut = jnp.where(scores[0] == tops[0][a], cut_a, cut)
        th1_ref[h] = cut
        w1_ref[h] = jnp.exp(scores[0] - max1) / z
        w2_ref[h] = e2


def peer_route(q, keys, *, tm):
    m = q.shape[0]
    nh = PEER_HEADS
    table = jax.ShapeDtypeStruct((nh, PEER_N_KEYS, m), F32)
    table_spec = pl.BlockSpec((nh, PEER_N_KEYS, tm), lambda i: (0, 0, i))
    return pl.pallas_call(
        _peer_route_kernel,
        grid=(m // tm,),
        in_specs=[pl.BlockSpec((tm, q.shape[1]), lambda i: (i, 0)),
                  pl.BlockSpec(keys.shape, lambda i: (0, 0, 0, 0))],
        out_specs=[table_spec] * 3,
        out_shape=[table] * 3,
        scratch_shapes=[pltpu.VMEM((_CAND_ROWS, tm), F32)],
        compiler_params=_cparams("parallel"),
        name="peer_route",
    )(q, keys)


def _peer_main_kernel(xb_ref, u_ref, vt_ref, th1_ref, w1_ref, w2_ref, x_ref, g_ref, b_ref,
                      o_ref, ob_ref, acc_ref, ht_ref, a_ref, *, te):
    e = pl.program_id(1)
    groups = te // PEER_N_KEYS
    tm = ht_ref.shape[1]

    @pl.when(e == 0)
    def _():
        acc_ref[...] = jnp.zeros_like(acc_ref)

    ht_ref[...] = _dot_nt(u_ref[...], xb_ref[...])
    for cc in range(groups):
        key1 = e * groups + cc
        rows = slice(cc * PEER_N_KEYS, (cc + 1) * PEER_N_KEYS)
        for t0 in range(0, tm, 2 * LANE_TILE):
            pair = slice(t0, t0 + 2 * LANE_TILE)
            th_rows = [th1_ref[h, pl.ds(key1, 1), pair] for h in range(PEER_HEADS)]
            w1_rows = [w1_ref[h, pl.ds(key1, 1), pair] for h in range(PEER_HEADS)]
            for half in range(2):
                lanes = slice(t0 + half * LANE_TILE, t0 + (half + 1) * LANE_TILE)
                sub = slice(half * LANE_TILE, (half + 1) * LANE_TILE)
                gate = jnp.zeros((PEER_N_KEYS, LANE_TILE), F32)
                for h in range(PEER_HEADS):
                    w2 = w2_ref[h, :, lanes]
                    gate = gate + jnp.where(w2 >= th_rows[h][:, sub], w2, 0.0) * w1_rows[h][:, sub]
                a_ref[rows, lanes] = (_gelu_tanh(ht_ref[rows, lanes]) * gate).astype(BF16)
    acc_ref[...] += _dot(vt_ref[...], a_ref[...])

    @pl.when(e == pl.num_programs(1) - 1)
    def _():
        for t0 in range(0, tm, LANE_TILE):
            tok = slice(t0, t0 + LANE_TILE)
            y = _layer_norm_rows(DEEPNORM_ALPHA * x_ref[tok, :] + acc_ref[:, tok].T, g_ref[...], b_ref[...])
            o_ref[tok, :] = y
            ob_ref[tok, :] = y.astype(BF16)


def peer_main(xb, x, u, vt, th1, w1, w2, g, b, *, tm, te):
    m, d = x.shape
    ne = u.shape[0]
    nh = PEER_HEADS
    kernel = functools.partial(_peer_main_kernel, te=te)
    once = pl.Buffered(1)
    table_spec = pl.BlockSpec((nh, PEER_N_KEYS, tm), lambda i, e: (0, 0, i))
    return pl.pallas_call(
        kernel,
        grid=(m // tm, ne // te),
        in_specs=[pl.BlockSpec((tm, d), lambda i, e: (i, 0), pipeline_mode=once),
                  pl.BlockSpec((te, d), lambda i, e: (e, 0)),
                  pl.BlockSpec((d, te), lambda i, e: (0, e)),
                  table_spec, table_spec, table_spec,
                  pl.BlockSpec((tm, d), lambda i, e: (i, 0), pipeline_mode=once),
                  pl.BlockSpec((1, d), lambda i, e: (0, 0)),
                  pl.BlockSpec((1, d), lambda i, e: (0, 0))],
        out_specs=[pl.BlockSpec((tm, d), lambda i, e: (i, 0), pipeline_mode=once),
                   pl.BlockSpec((tm, d), lambda i, e: (i, 0), pipeline_mode=once)],
        out_shape=[jax.ShapeDtypeStruct((m, d), F32), jax.ShapeDtypeStruct((m, d), BF16)],
        scratch_shapes=[pltpu.VMEM((d, tm), F32), pltpu.VMEM((te, tm), F32), pltpu.VMEM((te, tm), BF16)],
        compiler_params=_cparams("parallel", "arbitrary"),
        name="peer_main",
    )(xb, u, vt, th1, w1, w2, x, g.reshape(1, d), b.reshape(1, d))


TOKEN_TILE = 3072
GATE_TOKEN_TILE = 1024
PROJ_COL_TILE = 512
OUT_K_TILE = 512
OUT_ROW_TILE = 1024
PEER_TOKEN_TILE = 512
PEER_EXPERT_TILE = 1024


def kernel(x_prompt, x_sample, state_gdn_conv, state_gdn_s, state_mlstm_c, state_mlstm_n, state_mlstm_m,
           gdn_w_in, gdn_conv_w, gdn_a_log, gdn_dt_bias, gdn_norm_w, gdn_w_out,
           ml_w_in, ml_gate_b, ml_norm_w, ml_w_out,
           ln_mix_g, ln_mix_b, ln_ffn_g, ln_ffn_b,
           peer_w_q, peer_keys, peer_u, peer_v):
    bp, tp, d = x_prompt.shape
    bs, ts, _ = x_sample.shape
    np_tok = bp * tp
    ns_tok = bs * ts
    x = jnp.concatenate([x_prompt.reshape(np_tok, d), x_sample.reshape(ns_tok, d)], axis=0)
    xb = x.astype(BF16)
    cp = min(GDN_CHUNK, tp)
    cs = min(GDN_CHUNK, ts)

    n_gdn = state_gdn_s.shape[0]
    n_ml = state_mlstm_c.shape[0]
    zero_gdn = (jnp.zeros((1, bp) + state_gdn_conv.shape[2:], F32), jnp.zeros((1, bp) + state_gdn_s.shape[2:], F32))
    zero_ml = (jnp.zeros((1, bp) + state_mlstm_c.shape[2:], F32), jnp.zeros((1, bp) + state_mlstm_n.shape[2:], F32),
               jnp.zeros((1, bp) + state_mlstm_m.shape[2:], F32))
    p_gdn = s_gdn = p_ml = s_ml = None
    for layer in range(DEPTH):
        j = layer // 2
        prompt = dict(row0=0, nb=bp, t=tp, c=cp, slot=j)
        sample = dict(row0=np_tok, nb=bs, t=ts, c=cs, slot=j)
        if layer % 2 == 0:
            proj = matmul(xb, gdn_w_in, j, GDN_MAIN_DIM, tm=TOKEN_TILE, tn=PROJ_COL_TILE)
            gates, gates_t = gate_proj(xb, gdn_w_in[j, :, GDN_MAIN_DIM:], tm=GATE_TOKEN_TILE)
            shared = (gdn_conv_w[j], gdn_a_log[j], gdn_dt_bias[j], gdn_norm_w[j])
            o_p, p_gdn, (u_bf16, vt_bf16) = gdn_group(proj, gates, gates_t, *zero_gdn, 0, *shared, n_slots=n_gdn,
                                                      prev=p_gdn, tables=(peer_u, peer_v), table_layer=layer, **prompt)
            o_s, s_gdn, _ = gdn_group(proj, gates, gates_t, state_gdn_conv, state_gdn_s, j, *shared,
                                      n_slots=n_gdn, prev=s_gdn, **sample)
            w_out = gdn_w_out
        else:
            proj = matmul(xb, ml_w_in, j, ML_MAIN_DIM, tm=TOKEN_TILE, tn=PROJ_COL_TILE)
            gates, gates_t = gate_proj(xb, ml_w_in[j, :, ML_MAIN_DIM:], tm=GATE_TOKEN_TILE)
            shared = (ml_gate_b[j], ml_norm_w[j])
            o_p, p_ml, (u_bf16, vt_bf16) = mlstm_group(proj, gates, gates_t, *zero_ml, 0, *shared, n_slots=n_ml,
                                                       prev=p_ml, tables=(peer_u, peer_v), table_layer=layer, **prompt)
            o_s, s_ml, _ = mlstm_group(proj, gates, gates_t, state_mlstm_c, state_mlstm_n, state_mlstm_m, j, *shared,
                                       n_slots=n_ml, prev=s_ml, **sample)
            w_out = ml_w_out
        x, xb = out_proj_ln(o_p, o_s, w_out, j, x, ln_mix_g[layer], ln_mix_b[layer],
                            tm=OUT_ROW_TILE, tk=OUT_K_TILE)

        q = matmul(xb, peer_w_q, layer, peer_w_q.shape[2], tm=TOKEN_TILE, tn=PROJ_COL_TILE)
        tables = peer_route(q, peer_keys[layer], tm=PEER_TOKEN_TILE)
        x, xb = peer_main(xb, x, u_bf16, vt_bf16, *tables,
                          ln_ffn_g[layer], ln_ffn_b[layer], tm=PEER_TOKEN_TILE, te=PEER_EXPERT_TILE)

    y_prompt = x[:np_tok].reshape(bp, tp, d)
    y_sample = x[np_tok:].reshape(bs, ts, d)
    return (y_prompt, y_sample,
            p_gdn[0], p_gdn[1], p_ml[0], p_ml[1], p_ml[2].reshape(n_ml, bp, ML_HEADS),
            s_gdn[0], s_gdn[1], s_ml[0], s_ml[1], s_ml[2].reshape(n_ml, bs, ML_HEADS))
```
